```python
import jax, jax.numpy as jnp
from jax import lax
import numpy as np

D_MODEL = 1024
BATCH = 16
SEQ = 2048
DEPTH = 1
DEC_BATCH = 32
DEC_SEQ = 16
PAST_LEN = 4096

CHUNK = 64
D_CONV_A = D_MODEL
D_CONV_B = D_MODEL
K_A = 3
K_B = 31
N_GROUPS = 4
EXPERTS_PER_GROUP = 8
N_EXPERTS = N_GROUPS * EXPERTS_PER_GROUP
TOP_K_IN_GROUP = 2
D_EXPERT = D_MODEL // 2
RMS_EPS = 1e-6
LN_EPS = 1e-5
SPLITS = (D_CONV_A, 2 * D_CONV_A, 3 * D_CONV_A, 3 * D_CONV_A + D_CONV_B, 3 * D_CONV_A + 2 * D_CONV_B)
IN_COLS = 3 * D_CONV_A + 2 * D_CONV_B + 2 * D_MODEL

kernel_name = "hybrid_shortconv_conformer_hmoe_stream_step"


def rms_norm(x, g):
    x32 = x.astype(jnp.float32)
    y = x32 * lax.rsqrt(jnp.mean(x32 * x32, axis=-1, keepdims=True) + RMS_EPS)
    return y.astype(x.dtype) * g


def layer_norm(x, g, b):
    x32 = x.astype(jnp.float32)
    mu = jnp.mean(x32, axis=-1, keepdims=True)
    xc = x32 - mu
    var = jnp.mean(xc * xc, axis=-1, keepdims=True)
    return (xc * lax.rsqrt(var + LN_EPS)).astype(x.dtype) * g + b


def causal_depthwise_conv(u, buf, w):
    k = w.shape[0]
    full = jnp.concatenate([buf.astype(u.dtype), u], axis=1)
    y = lax.conv_general_dilated(full, w[:, None, :].astype(u.dtype), window_strides=(1,), padding='VALID',
                                 dimension_numbers=('NWC', 'WIO', 'NWC'), feature_group_count=u.shape[-1])
    return y, full[:, full.shape[1] - (k - 1):, :]


def mixer_block(x, buf_a, buf_b, norm1_g, w_in, gate_b, conv_a_w, w_out_a,
                conv_b_w, conv_b_b, ln_b_g, ln_b_b, w_out_b, b_out_b, w_o):
    xn = rms_norm(x, norm1_g)
    proj = jnp.einsum('btd,dc->btc', xn, w_in)
    b_a, c_a, h_a, glu_a, glu_b, g_in = jnp.split(proj, SPLITS, axis=-1)
    conv_a, new_a = causal_depthwise_conv(c_a * h_a, buf_a, conv_a_w)
    y_a = jnp.einsum('btc,cd->btd', b_a * conv_a, w_out_a)
    u_b = glu_a * jax.nn.sigmoid(glu_b)
    conv_b, new_b = causal_depthwise_conv(u_b, buf_b, conv_b_w)
    v_b = jax.nn.silu(layer_norm(conv_b + conv_b_b, ln_b_g, ln_b_b))
    y_b = jnp.einsum('btc,cd->btd', v_b, w_out_b) + b_out_b
    g_a, g_b = jnp.split(jax.nn.sigmoid(g_in + gate_b), 2, axis=-1)
    out = jnp.einsum('btd,de->bte', g_a * y_a + g_b * y_b, w_o)
    return x + out, new_a, new_b


def hier_moe(x, norm2_g, rg_w, rg_b, re_w, re_b, w_gate, w_up, w_down):
    bsz, t, d = x.shape
    xn = rms_norm(x, norm2_g).reshape(bsz * t, d)
    lg = jnp.einsum('nd,dg->ng', xn, rg_w).astype(jnp.float32) + rg_b.astype(jnp.float32)
    p_group = jax.nn.softmax(lg, axis=-1)
    p_top, grp = lax.top_k(p_group, 1)
    le = (jnp.einsum('nd,de->ne', xn, re_w).astype(jnp.float32) + re_b.astype(jnp.float32))
    le = le.reshape(-1, N_GROUPS, EXPERTS_PER_GROUP)
    le_sel = jnp.take_along_axis(le, grp[:, :, None], axis=1)[:, 0]
    top_l, top_i = lax.top_k(le_sel, TOP_K_IN_GROUP)
    w_top = jax.nn.softmax(top_l, axis=-1) * p_top
    expert_id = grp * EXPERTS_PER_GROUP + top_i
    combine = jnp.sum(jax.nn.one_hot(expert_id, N_EXPERTS, dtype=jnp.float32) * w_top[..., None], axis=1)
    combine = combine.astype(x.dtype)
    out = jnp.zeros_like(xn)
    for e in range(N_EXPERTS):
        hdn = jax.nn.silu(xn @ w_gate[e]) * (xn @ w_up[e])
        out = out + combine[:, e:e + 1] * (hdn @ w_down[e])
    return x + out.reshape(bsz, t, d)


def trunk(x, bufs_a, bufs_b, norm1_g, w_in, gate_b, conv_a_w, w_out_a, conv_b_w, conv_b_b,
          ln_b_g, ln_b_b, w_out_b, b_out_b, w_o, norm2_g, router_group_w, router_group_b,
          router_expert_w, router_expert_b, exp_w_gate, exp_w_up, exp_w_down, final_norm_g):
    new_a, new_b = [], []
    for l in range(DEPTH):
        x, na, nb = mixer_block(x, bufs_a[l], bufs_b[l], norm1_g[l], w_in[l], gate_b[l], conv_a_w[l],
                                w_out_a[l], conv_b_w[l], conv_b_b[l], ln_b_g[l], ln_b_b[l],
                                w_out_b[l], b_out_b[l], w_o[l])
        x = hier_moe(x, norm2_g[l], router_group_w[l], router_group_b[l], router_expert_w[l],
                     router_expert_b[l], exp_w_gate[l], exp_w_up[l], exp_w_down[l])
        new_a.append(na)
        new_b.append(nb)
    return rms_norm(x, final_norm_g), jnp.stack(new_a), jnp.stack(new_b)


def setup_inputs(seed: int = 0) -> dict:
    key = jax.random.key(seed)
    ks = jax.random.split(key, 32)
    f32 = jnp.float32
    nrm = lambda k, shape, s: jax.random.normal(k, shape, f32) * s
    L = DEPTH
    return {
        "x_prompt": nrm(ks[0], (BATCH, SEQ, D_MODEL), 1.0),
        "x_sample": nrm(ks[1], (DEC_BATCH, DEC_SEQ, D_MODEL), 1.0),
        "cache_conv_a": nrm(ks[2], (L, DEC_BATCH, K_A - 1, D_CONV_A), 1.0),
        "cache_conv_b": nrm(ks[3], (L, DEC_BATCH, K_B - 1, D_CONV_B), 1.0),
        "norm1_g": 1.0 + nrm(ks[4], (L, D_MODEL), 0.02),
        "w_in": nrm(ks[5], (L, D_MODEL, IN_COLS), D_MODEL ** -0.5),
        "gate_b": nrm(ks[6], (L, 2 * D_MODEL), 0.02),
        "conv_a_w": nrm(ks[7], (L, K_A, D_CONV_A), K_A ** -0.5),
        "w_out_a": nrm(ks[8], (L, D_CONV_A, D_MODEL), D_CONV_A ** -0.5),
        "conv_b_w": nrm(ks[9], (L, K_B, D_CONV_B), K_B ** -0.5),
        "conv_b_b": nrm(ks[10], (L, D_CONV_B), 0.02),
        "ln_b_g": 1.0 + nrm(ks[11], (L, D_CONV_B), 0.02),
        "ln_b_b": nrm(ks[12], (L, D_CONV_B), 0.02),
        "w_out_b": nrm(ks[13], (L, D_CONV_B, D_MODEL), D_CONV_B ** -0.5),
        "b_out_b": nrm(ks[14], (L, D_MODEL), 0.02),
        "w_o": nrm(ks[15], (L, D_MODEL, D_MODEL), D_MODEL ** -0.5),
        "norm2_g": 1.0 + nrm(ks[16], (L, D_MODEL), 0.02),
        "router_group_w": nrm(ks[17], (L, D_MODEL, N_GROUPS), D_MODEL ** -0.5),
        "router_group_b": nrm(ks[18], (L, N_GROUPS), 0.01),
        "router_expert_w": nrm(ks[19], (L, D_MODEL, N_EXPERTS), D_MODEL ** -0.5),
        "router_expert_b": nrm(ks[20], (L, N_EXPERTS), 0.01),
        "exp_w_gate": nrm(ks[21], (L, N_EXPERTS, D_MODEL, D_EXPERT), D_MODEL ** -0.5),
        "exp_w_up": nrm(ks[22], (L, N_EXPERTS, D_MODEL, D_EXPERT), D_MODEL ** -0.5),
        "exp_w_down": nrm(ks[23], (L, N_EXPERTS, D_EXPERT, D_MODEL), D_EXPERT ** -0.5),
        "final_norm_g": 1.0 + nrm(ks[24], (D_MODEL,), 0.02),
    }


def reference(x_prompt, x_sample, cache_conv_a, cache_conv_b, norm1_g, w_in, gate_b, conv_a_w, w_out_a,
              conv_b_w, conv_b_b, ln_b_g, ln_b_b, w_out_b, b_out_b, w_o, norm2_g, router_group_w,
              router_group_b, router_expert_w, router_expert_b, exp_w_gate, exp_w_up, exp_w_down,
              final_norm_g):
    weights = (norm1_g, w_in, gate_b, conv_a_w, w_out_a, conv_b_w, conv_b_b, ln_b_g, ln_b_b, w_out_b,
               b_out_b, w_o, norm2_g, router_group_w, router_group_b, router_expert_w, router_expert_b,
               exp_w_gate, exp_w_up, exp_w_down, final_norm_g)
    bsz = x_prompt.shape[0]
    zeros_a = jnp.zeros((DEPTH, bsz, K_A - 1, D_CONV_A), x_prompt.dtype)
    zeros_b = jnp.zeros((DEPTH, bsz, K_B - 1, D_CONV_B), x_prompt.dtype)
    y_prompt, new_a_prompt, new_b_prompt = trunk(x_prompt, zeros_a, zeros_b, *weights)
    y_sample, new_a_sample, new_b_sample = trunk(x_sample, cache_conv_a, cache_conv_b, *weights)
    return (y_prompt, y_sample, new_a_prompt, new_a_sample, new_b_prompt, new_b_sample)
```

```python
import functools

import jax
import jax.numpy as jnp
from jax import lax
from jax.experimental import pallas as pl
from jax.experimental.pallas import tpu as pltpu

D_MODEL = 1024
K_A = 3
K_B = 31
N_GROUPS = 4
EXPERTS_PER_GROUP = 8
N_EXPERTS = N_GROUPS * EXPERTS_PER_GROUP
D_EXPERT = D_MODEL // 2
RMS_EPS = 1e-6
LN_EPS = 1e-5

SUBLANES = 8
LANES = 128
FRAME_ROWS = D_MODEL // LANES
assert FRAME_ROWS == SUBLANES
HIST_A = SUBLANES
assert HIST_A >= K_A - 1

VMEM_LIMIT_BYTES = 58 * 1024 * 1024

_F32 = jnp.float32
_BF16 = jnp.bfloat16


def _rms_norm(x, g):
    return x * lax.rsqrt(jnp.mean(x * x, axis=-1, keepdims=True) + RMS_EPS) * g


def _dot(a, b):
    return jnp.dot(a, b, preferred_element_type=_F32)


def _to_frame_major(dst_ref, first_frame, x):
    n = x.shape[0]
    for j in range(FRAME_ROWS):
        dst_ref[pl.ds(first_frame * FRAME_ROWS + j, n, stride=FRAME_ROWS), :] = (
            x[:, j * LANES:(j + 1) * LANES])


def _from_frame_major(src_ref, first_frame, n):
    return jnp.concatenate(
        [src_ref[pl.ds(first_frame * FRAME_ROWS + j, n, stride=FRAME_ROWS), :]
         for j in range(FRAME_ROWS)], axis=-1)


def _mixer_kernel(*refs, nb, tt, nt, has_cache):
    if has_cache:
        x_ref, ca_ref, cb_ref = refs[:3]
        refs = refs[3:]
    else:
        x_ref = refs[0]
        refs = refs[1:]
    (n1g_ref, win_ref, gateb_ref, caw_ref, woa_ref, cbw_ref, cbb_ref, lng_ref, lnb_ref,
     wob_ref, bob_ref, wo_ref, x1_ref, newa_ref, newb_ref, sa_ref, sb_ref, cv_ref) = refs
    d = D_MODEL
    m = nb * tt
    hb = K_B - 1
    seq_frames = hb + tt
    t = pl.program_id(1)

    @pl.when(t == 0)
    def _():
        if has_cache:
            sa_ref[:, HIST_A - (K_A - 1):HIST_A, :] = ca_ref[0]
            for s in range(nb):
                _to_frame_major(sb_ref, s * seq_frames, cb_ref[0, s])
        else:
            sa_ref[:, 0:HIST_A, :] = jnp.zeros((nb, HIST_A, d), _F32)
            for s in range(nb):
                sb_ref[pl.ds(s * seq_frames * FRAME_ROWS, hb * FRAME_ROWS), :] = (
                    jnp.zeros((hb * FRAME_ROWS, LANES), _F32))

    x = x_ref[...].reshape(m, d)
    xn = _rms_norm(x, n1g_ref[...]).astype(_BF16)

    z = _dot(xn, win_ref[:, d:2 * d]) * _dot(xn, win_ref[:, 2 * d:3 * d])
    sa_ref[:, HIST_A:HIST_A + tt, :] = z.reshape(nb, tt, d)
    conv_a = None
    for k in range(K_A):
        lo = HIST_A - (K_A - 1) + k
        term = sa_ref[:, lo:lo + tt, :] * caw_ref[k:k + 1, :]
        conv_a = term if conv_a is None else conv_a + term
    newa_ref[0] = sa_ref[:, HIST_A + tt - (K_A - 1):HIST_A + tt, :]
    if nt > 1:
        sa_ref[:, HIST_A - (K_A - 1):HIST_A, :] = sa_ref[:, HIST_A + tt - (K_A - 1):HIST_A + tt, :]
    b_a = _dot(xn, win_ref[:, 0:d])
    y_a = _dot((b_a * conv_a.reshape(m, d)).astype(_BF16), woa_ref[...])

    u = _dot(xn, win_ref[:, 3 * d:4 * d]) * jax.nn.sigmoid(_dot(xn, win_ref[:, 4 * d:5 * d]))
    u3 = u.reshape(nb, tt, d)
    if tt >= hb:
        newb_ref[0] = u3[:, tt - hb:tt, :]
    else:
        newb_ref[0, :, 0:hb - tt, :] = cb_ref[0, :, tt:hb, :]
        newb_ref[0, :, hb - tt:hb, :] = u3
    for s in range(nb):
        _to_frame_major(sb_ref, s * seq_frames + hb, u[s * tt:(s + 1) * tt, :])
    for s in range(nb):
        acc = None
        for k in range(K_B):
            win = sb_ref[pl.ds((s * seq_frames + k) * FRAME_ROWS, tt * FRAME_ROWS), :]
            term = win.reshape(tt, FRAME_ROWS, LANES) * cbw_ref[k]
            acc = term if acc is None else acc + term
        acc = acc + cbb_ref[...]
        cv_ref[pl.ds(s * tt * FRAME_ROWS, tt * FRAME_ROWS), :] = acc.reshape(tt * FRAME_ROWS, LANES)
    if nt > 1:
        sb_ref[pl.ds(0, hb * FRAME_ROWS), :] = sb_ref[pl.ds(tt * FRAME_ROWS, hb * FRAME_ROWS), :]
    cb = _from_frame_major(cv_ref, 0, m)
    mu = jnp.mean(cb, axis=-1, keepdims=True)
    xc = cb - mu
    var = jnp.mean(xc * xc, axis=-1, keepdims=True)
    v_b = jax.nn.silu(xc * lax.rsqrt(var + LN_EPS) * lng_ref[...] + lnb_ref[...])
    y_b = _dot(v_b.astype(_BF16), wob_ref[...]) + bob_ref[...]

    g_a = jax.nn.sigmoid(_dot(xn, win_ref[:, 5 * d:6 * d]) + gateb_ref[:, 0:d])
    g_b = jax.nn.sigmoid(_dot(xn, win_ref[:, 6 * d:7 * d]) + gateb_ref[:, d:2 * d])
    mix = (g_a * y_a + g_b * y_b).astype(_BF16)
    x1 = x + _dot(mix, wo_ref[...])
    x1_ref[...] = x1.reshape(nb, tt, d)


def _const_spec(shape):
    zeros = (0,) * len(shape)
    return pl.BlockSpec(shape, lambda *_: zeros, pipeline_mode=pl.Buffered(1))


def _mixer(x, caches, weights, *, nb, tt):
    b, s, d = x.shape
    assert b % nb == 0 and s % tt == 0
    nt = s // tt
    has_cache = caches is not None
    assert (nt == 1) if has_cache else (nb == 1 and tt >= K_B - 1)
    in_specs = [pl.BlockSpec((nb, tt, d), lambda i, j: (i, j, 0))]
    args = [x]
    if has_cache:
        in_specs += [pl.BlockSpec((1, nb, K_A - 1, d), lambda i, j: (0, i, 0, 0)),
                     pl.BlockSpec((1, nb, K_B - 1, d), lambda i, j: (0, i, 0, 0))]
        args += list(caches)
    in_specs += [_const_spec(w.shape) for w in weights]
    args += list(weights)
    out_shape = (jax.ShapeDtypeStruct((b, s, d), _F32),
                 jax.ShapeDtypeStruct((1, b, K_A - 1, d), _F32),
                 jax.ShapeDtypeStruct((1, b, K_B - 1, d), _F32))
    out_specs = (pl.BlockSpec((nb, tt, d), lambda i, j: (i, j, 0)),
                 pl.BlockSpec((1, nb, K_A - 1, d), lambda i, j: (0, i, 0, 0)),
                 pl.BlockSpec((1, nb, K_B - 1, d), lambda i, j: (0, i, 0, 0)))
    return pl.pallas_call(
        functools.partial(_mixer_kernel, nb=nb, tt=tt, nt=nt, has_cache=has_cache),
        grid=(b // nb, nt),
        in_specs=in_specs,
        out_specs=out_specs,
        out_shape=out_shape,
        scratch_shapes=[pltpu.VMEM((nb, HIST_A + tt, d), _F32),
                        pltpu.VMEM((nb * (K_B - 1 + tt) * FRAME_ROWS, LANES), _F32),
                        pltpu.VMEM((nb * tt * FRAME_ROWS, LANES), _F32)],
        compiler_params=pltpu.CompilerParams(
            dimension_semantics=("arbitrary", "arbitrary"),
            vmem_limit_bytes=VMEM_LIMIT_BYTES),
        name="mixer_cache" if has_cache else "mixer_prompt",
    )(*args)


E_PAD = 48
PREFIX_CHUNK = 256
COUNT_SPLIT = 64


def _router_kernel(x1_ref, n2g_ref, wr_ref, rb_ref, pos_ref, wts_ref, offs_ref, *, t):
    assert t % PREFIX_CHUNK == 0 and 2 * t <= COUNT_SPLIT * 256
    xn = _rms_norm(x1_ref[0], n2g_ref[...]).astype(_BF16)
    lg = lax.dot_general(wr_ref[...], xn, (((1,), (1,)), ((), ())),
                         preferred_element_type=_F32) + rb_ref[...]
    row = [lg[r:r + 1, :] for r in range(N_GROUPS + N_EXPERTS)]

    grp = jnp.zeros((1, t), jnp.int32)
    best = row[0]
    for g in range(1, N_GROUPS):
        gt = row[g] > best
        grp = jnp.where(gt, g, grp)
        best = jnp.where(gt, row[g], best)
    denom = None
    for g in range(N_GROUPS):
        e = jnp.exp(row[g] - best)
        denom = e if denom is None else denom + e
    p_top = 1.0 / denom

    sel = []
    for j in range(EXPERTS_PER_GROUP):
        v = row[N_GROUPS + j]
        for g in range(1, N_GROUPS):
            v = jnp.where(grp == g, row[N_GROUPS + g * EXPERTS_PER_GROUP + j], v)
        sel.append(v)
    i1 = jnp.zeros((1, t), jnp.int32)
    v1 = sel[0]
    for j in range(1, EXPERTS_PER_GROUP):
        gt = sel[j] > v1
        i1 = jnp.where(gt, j, i1)
        v1 = jnp.where(gt, sel[j], v1)
    i2 = jnp.where(i1 == 0, 1, 0).astype(jnp.int32)
    v2 = jnp.where(i1 == 0, sel[1], sel[0])
    for j in range(1, EXPERTS_PER_GROUP):
        gt = (sel[j] > v2) & (i1 != j) & (i2 != j)
        i2 = jnp.where(gt, j, i2)
        v2 = jnp.where(gt, sel[j], v2)
    e2x = jnp.exp(v2 - v1)
    ssum = 1.0 + e2x
    w1 = (1.0 / ssum) * p_top
    w2 = (e2x / ssum) * p_top
    e1 = grp * EXPERTS_PER_GROUP + i1
    e2 = grp * EXPERTS_PER_GROUP + i2

    eidx = lax.broadcasted_iota(jnp.int32, (E_PAD, t), 0)
    oh1 = eidx == e1
    oh2 = eidx == e2
    hot = (oh1 | oh2).astype(_F32)
    s_i = lax.broadcasted_iota(jnp.int32, (PREFIX_CHUNK, PREFIX_CHUNK), 0)
    t_i = lax.broadcasted_iota(jnp.int32, (PREFIX_CHUNK, PREFIX_CHUNK), 1)
    before = (s_i < t_i).astype(_BF16)
    carry = jnp.zeros((E_PAD, 1), _F32)
    prefix = []
    for c in range(t // PREFIX_CHUNK):
        hc = hot[:, c * PREFIX_CHUNK:(c + 1) * PREFIX_CHUNK]
        prefix.append(_dot(hc.astype(_BF16), before) + carry)
        carry = carry + jnp.sum(hc, axis=1, keepdims=True)
    prefix = jnp.concatenate(prefix, axis=1)
    counts = jnp.broadcast_to(carry, (E_PAD, LANES))
    c_hi = jnp.floor(counts * (1.0 / COUNT_SPLIT))
    c_lo = counts - c_hi * COUNT_SPLIT
    a_i = lax.broadcasted_iota(jnp.int32, (E_PAD, E_PAD), 0)
    b_i = lax.broadcasted_iota(jnp.int32, (E_PAD, E_PAD), 1)
    lower = (b_i < a_i).astype(_BF16)
    offs = _dot(lower, c_hi.astype(_BF16)) * COUNT_SPLIT + _dot(lower, c_lo.astype(_BF16))
    offs_ref[0] = offs[:, 0:1].astype(jnp.int32)
    slot = offs[:, 0:1] + prefix
    pos1 = jnp.sum(jnp.where(oh1, slot, 0.0), axis=0, keepdims=True)
    pos2 = jnp.sum(jnp.where(oh2, slot, 0.0), axis=0, keepdims=True)
    pos_ref[0, :, 0:t] = pos1.astype(jnp.int32)
    pos_ref[0, :, t:2 * t] = pos2.astype(jnp.int32)
    wts_ref[0, :, 0:t] = w1
    wts_ref[0, :, t:2 * t] = w2


def _router(x1, n2g, wr, rb):
    nt, t, d = x1.shape
    pos, wts, offs = pl.pallas_call(
        functools.partial(_router_kernel, t=t),
        grid=(nt,),
        in_specs=[pl.BlockSpec((1, t, d), lambda k: (k, 0, 0)),
                  _const_spec(n2g.shape), _const_spec(wr.shape), _const_spec(rb.shape)],
        out_specs=(pl.BlockSpec((1, 1, 2 * t), lambda k: (k, 0, 0)),
                   pl.BlockSpec((1, 1, 2 * t), lambda k: (k, 0, 0)),
                   pl.BlockSpec((1, E_PAD, 1), lambda k: (k, 0, 0))),
        out_shape=(jax.ShapeDtypeStruct((nt, 1, 2 * t), jnp.int32),
                   jax.ShapeDtypeStruct((nt, 1, 2 * t), _F32),
                   jax.ShapeDtypeStruct((nt, E_PAD, 1), jnp.int32)),
        compiler_params=pltpu.CompilerParams(
            dimension_semantics=("arbitrary",), vmem_limit_bytes=VMEM_LIMIT_BYTES),
        name=f"router_t{t}",
    )(x1, n2g, wr, rb)
    return pos, wts, offs.reshape(nt, E_PAD)


MOE_SUB = 256
TOKENS_PER_STEP = 8


def _moe_kernel(offs_ref, pos_ref, wts_ref, x1_ref, n2g_ref, fg_ref, wg_ref, wu_ref, wd_ref,
                out_ref, xy_ref, stage_ref, *, t, blk):
    d = D_MODEL
    k = pl.program_id(0)
    e = pl.program_id(1)
    sub = min(MOE_SUB, t)
    n_sub = t // sub

    @pl.when(e == 0)
    def _scatter():
        xy_ref[pl.ds(2 * t * FRAME_ROWS, blk * FRAME_ROWS), :] = (
            jnp.zeros((blk * FRAME_ROWS, LANES), _F32))

        def sub_body(si, carry):
            r0 = pl.multiple_of(si * sub, sub)
            xn = _rms_norm(x1_ref[0, pl.ds(r0, sub), :], n2g_ref[...])
            _to_frame_major(stage_ref, 0, xn)

            def tok_body(ti, c):
                for i in range(TOKENS_PER_STEP):
                    tok = ti * TOKENS_PER_STEP + i
                    v = stage_ref[pl.ds(pl.multiple_of(tok * FRAME_ROWS, FRAME_ROWS), FRAME_ROWS), :]
                    p1 = pos_ref[0, 0, r0 + tok]
                    p2 = pos_ref[0, 0, t + r0 + tok]
                    xy_ref[pl.ds(pl.multiple_of(p1 * FRAME_ROWS, FRAME_ROWS), FRAME_ROWS), :] = v
                    xy_ref[pl.ds(pl.multiple_of(p2 * FRAME_ROWS, FRAME_ROWS), FRAME_ROWS), :] = v
                return c

            lax.fori_loop(0, sub // TOKENS_PER_STEP, tok_body, 0)
            return carry

        lax.fori_loop(0, n_sub, sub_body, 0)

    start = offs_ref[k, e]
    cnt = offs_ref[k, e + 1] - start
    n_blk = lax.div(cnt + (blk - 1), blk)

    def blk_body(bi, carry):
        r0 = start + bi * blk
        x = _from_frame_major(xy_ref, r0, blk)
        xb = x.astype(_BF16)
        h = jax.nn.silu(_dot(xb, wg_ref[0])) * _dot(xb, wu_ref[0])
        y = _dot(h.astype(_BF16), wd_ref[0])
        live = lax.broadcasted_iota(jnp.int32, (blk, 1), 0) < (cnt - bi * blk)
        _to_frame_major(xy_ref, r0, jnp.where(live, y, x))
        return carry

    lax.fori_loop(0, n_blk, blk_body, 0)

    @pl.when(e == N_EXPERTS - 1)
    def _combine():
        def sub_body(si, carry):
            r0 = pl.multiple_of(si * sub, sub)

            def tok_body(ti, c):
                for i in range(TOKENS_PER_STEP):
                    tok = ti * TOKENS_PER_STEP + i
                    p1 = pos_ref[0, 0, r0 + tok]
                    p2 = pos_ref[0, 0, t + r0 + tok]
                    y1 = xy_ref[pl.ds(pl.multiple_of(p1 * FRAME_ROWS, FRAME_ROWS), FRAME_ROWS), :]
                    y2 = xy_ref[pl.ds(pl.multiple_of(p2 * FRAME_ROWS, FRAME_ROWS), FRAME_ROWS), :]
                    v = wts_ref[0, 0, r0 + tok] * y1 + wts_ref[0, 0, t + r0 + tok] * y2
                    stage_ref[pl.ds(pl.multiple_of(tok * FRAME_ROWS, FRAME_ROWS), FRAME_ROWS), :] = v
                return c

            lax.fori_loop(0, sub // TOKENS_PER_STEP, tok_body, 0)
            x2 = x1_ref[0, pl.ds(r0, sub), :] + _from_frame_major(stage_ref, 0, sub)
            out_ref[0, pl.ds(r0, sub), :] = _rms_norm(x2, fg_ref[...])
            return carry

        lax.fori_loop(0, n_sub, sub_body, 0)


def _moe(x1, offs, pos, wts, n2g, fg, wg, wu, wd, *, blk):
    nt, t, d = x1.shape
    rows = 2 * t + blk
    grid_spec = pltpu.PrefetchScalarGridSpec(
        num_scalar_prefetch=1,
        grid=(nt, N_EXPERTS),
        in_specs=[
            pl.BlockSpec((1, 1, 2 * t), lambda k, e, o: (k, 0, 0), memory_space=pltpu.SMEM),
            pl.BlockSpec((1, 1, 2 * t), lambda k, e, o: (k, 0, 0), memory_space=pltpu.SMEM),
            pl.BlockSpec((1, t, d), lambda k, e, o: (k, 0, 0), pipeline_mode=pl.Buffered(1)),
            _const_spec(n2g.shape), _const_spec(fg.shape),
            pl.BlockSpec((1, d, D_EXPERT), lambda k, e, o: (e, 0, 0)),
            pl.BlockSpec((1, d, D_EXPERT), lambda k, e, o: (e, 0, 0)),
            pl.BlockSpec((1, D_EXPERT, d), lambda k, e, o: (e, 0, 0)),
        ],
        out_specs=pl.BlockSpec((1, t, d), lambda k, e, o: (k, 0, 0)),
        scratch_shapes=[pltpu.VMEM((rows * FRAME_ROWS, LANES), _F32),
                        pltpu.VMEM((min(MOE_SUB, t) * FRAME_ROWS, LANES), _F32)],
    )
    return pl.pallas_call(
        functools.partial(_moe_kernel, t=t, blk=blk),
        grid_spec=grid_spec,
        out_shape=jax.ShapeDtypeStruct((nt, t, d), _F32),
        compiler_params=pltpu.CompilerParams(
            dimension_semantics=("arbitrary", "arbitrary"), vmem_limit_bytes=VMEM_LIMIT_BYTES),
        name=f"moe_t{t}",
    )(offs, pos, wts, x1, n2g, fg, wg, wu, wd)


def _route_and_moe(x1, router_w, moe_w, *, blk):
    pos, wts, offs = _router(x1, *router_w)
    return _moe(x1, offs, pos, wts, *moe_w, blk=blk)


def kernel(x_prompt, x_sample, cache_conv_a, cache_conv_b, norm1_g, w_in, gate_b, conv_a_w, w_out_a,
           conv_b_w, conv_b_b, ln_b_g, ln_b_b, w_out_b, b_out_b, w_o, norm2_g, router_group_w,
           router_group_b, router_expert_w, router_expert_b, exp_w_gate, exp_w_up, exp_w_down,
           final_norm_g):
    d = D_MODEL
    mixer_w = (norm1_g, w_in[0].astype(_BF16), gate_b, conv_a_w[0], w_out_a[0].astype(_BF16),
               conv_b_w[0].reshape(K_B, FRAME_ROWS, LANES), conv_b_b.reshape(FRAME_ROWS, LANES),
               ln_b_g, ln_b_b, w_out_b[0].astype(_BF16), b_out_b, w_o[0].astype(_BF16))
    n_router = N_GROUPS + N_EXPERTS
    wr = jnp.concatenate([router_group_w[0].T, router_expert_w[0].T,
                          jnp.zeros((E_PAD - n_router, d), _F32)], axis=0).astype(_BF16)
    rb = jnp.concatenate([router_group_b[0], router_expert_b[0],
                          jnp.zeros((E_PAD - n_router,), _F32)]).reshape(E_PAD, 1)
    router_w = (norm2_g, wr, rb)
    moe_w = (norm2_g, final_norm_g.reshape(1, d), exp_w_gate[0].astype(_BF16),
             exp_w_up[0].astype(_BF16), exp_w_down[0].astype(_BF16))

    x1_p, na_p, nb_p = _mixer(x_prompt, None, mixer_w, nb=1, tt=256)
    x1_s, na_s, nb_s = _mixer(x_sample, (cache_conv_a, cache_conv_b), mixer_w, nb=16, tt=16)
    y_p = _route_and_moe(x1_p, router_w, moe_w, blk=160)
    bs, ts, _ = x_sample.shape
    y_s = _route_and_moe(x1_s.reshape(1, bs * ts, d), router_w, moe_w, blk=64)
    return (y_p, y_s.reshape(bs, ts, d), na_p, na_s, nb_p, nb_s)
```

```python
import functools

import jax
import jax.numpy as jnp
from jax import lax
from jax.experimental import pallas as pl
from jax.experimental.pallas import tpu as pltpu

D_MODEL = 1024
K_A = 3
K_B = 31
N_GROUPS = 4
EXPERTS_PER_GROUP = 8
N_EXPERTS = N_GROUPS * EXPERTS_PER_GROUP
D_EXPERT = D_MODEL // 2
RMS_EPS = 1e-6
LN_EPS = 1e-5

SUBLANES = 8
LANES = 128
FRAME_ROWS = D_MODEL // LANES
assert FRAME_ROWS == SUBLANES
HIST_A = SUBLANES
assert HIST_A >= K_A - 1

VMEM_LIMIT_BYTES = 58 * 1024 * 1024

_F32 = jnp.float32
_BF16 = jnp.bfloat16


def _rms_norm(x, g):
    return x * lax.rsqrt(jnp.mean(x * x, axis=-1, keepdims=True) + RMS_EPS) * g


def _dot(a, b):
    return jnp.dot(a, b, preferred_element_type=_F32)


def _to_frame_major(dst_ref, first_frame, x):
    n = x.shape[0]
    for j in range(FRAME_ROWS):
        dst_ref[pl.ds(first_frame * FRAME_ROWS + j, n, stride=FRAME_ROWS), :] = (
            x[:, j * LANES:(j + 1) * LANES])


def _from_frame_major(src_ref, first_frame, n):
    return jnp.concatenate(
        [src_ref[pl.ds(first_frame * FRAME_ROWS + j, n, stride=FRAME_ROWS), :]
         for j in range(FRAME_ROWS)], axis=-1)


def _mixer_kernel(*refs, nb, tt, nt, has_cache):
    if has_cache:
        x_ref, ca_ref, cb_ref = refs[:3]
        refs = refs[3:]
    else:
        x_ref = refs[0]
        refs = refs[1:]
    (n1g_ref, win_ref, gateb_ref, caw_ref, woa_ref, cbw_ref, cbb_ref, lng_ref, lnb_ref,
     wob_ref, bob_ref, wo_ref, x1_ref, newa_ref, newb_ref, sa_ref, sb_ref, cv_ref) = refs
    d = D_MODEL
    m = nb * tt
    hb = K_B - 1
    seq_frames = hb + tt
    t = pl.program_id(1)

    @pl.when(t == 0)
    def _():
        if has_cache:
            sa_ref[:, HIST_A - (K_A - 1):HIST_A, :] = ca_ref[0]
            for s in range(nb):
                _to_frame_major(sb_ref, s * seq_frames, cb_ref[0, s])
        else:
            sa_ref[:, 0:HIST_A, :] = jnp.zeros((nb, HIST_A, d), _F32)
            for s in range(nb):
                sb_ref[pl.ds(s * seq_frames * FRAME_ROWS, hb * FRAME_ROWS), :] = (
                    jnp.zeros((hb * FRAME_ROWS, LANES), _F32))

    x = x_ref[...].reshape(m, d)
    xn = _rms_norm(x, n1g_ref[...]).astype(_BF16)

    z = _dot(xn, win_ref[:, d:2 * d]) * _dot(xn, win_ref[:, 2 * d:3 * d])
    sa_ref[:, HIST_A:HIST_A + tt, :] = z.reshape(nb, tt, d)
    conv_a = None
    for k in range(K_A):
        lo = HIST_A - (K_A - 1) + k
        term = sa_ref[:, lo:lo + tt, :] * caw_ref[k:k + 1, :]
        conv_a = term if conv_a is None else conv_a + term
    newa_ref[0] = sa_ref[:, HIST_A + tt - (K_A - 1):HIST_A + tt, :]
    if nt > 1:
        sa_ref[:, HIST_A - (K_A - 1):HIST_A, :] = sa_ref[:, HIST_A + tt - (K_A - 1):HIST_A + tt, :]
    b_a = _dot(xn, win_ref[:, 0:d])
    y_a = _dot((b_a * conv_a.reshape(m, d)).astype(_BF16), woa_ref[...])

    u = _dot(xn, win_ref[:, 3 * d:4 * d]) * jax.nn.sigmoid(_dot(xn, win_ref[:, 4 * d:5 * d]))
    u3 = u.reshape(nb, tt, d)
    if tt >= hb:
        newb_ref[0] = u3[:, tt - hb:tt, :]
    else:
        newb_ref[0, :, 0:hb - tt, :] = cb_ref[0, :, tt:hb, :]
        newb_ref[0, :, hb - tt:hb, :] = u3
    for s in range(nb):
        _to_frame_major(sb_ref, s * seq_frames + hb, u[s * tt:(s + 1) * tt, :])
    for s in range(nb):
        acc = None
        for k in range(K_B):
            win = sb_ref[pl.ds((s * seq_frames + k) * FRAME_ROWS, tt * FRAME_ROWS), :]
            term = win.reshape(tt, FRAME_ROWS, LANES) * cbw_ref[k]
            acc = term if acc is None else acc + term
        acc = acc + cbb_ref[...]
        cv_ref[pl.ds(s * tt * FRAME_ROWS, tt * FRAME_ROWS), :] = acc.reshape(tt * FRAME_ROWS, LANES)
    if nt > 1:
        sb_ref[pl.ds(0, hb * FRAME_ROWS), :] = sb_ref[pl.ds(tt * FRAME_ROWS, hb * FRAME_ROWS), :]
    cb = _from_frame_major(cv_ref, 0, m)
    mu = jnp.mean(cb, axis=-1, keepdims=True)
    xc = cb - mu
    var = jnp.mean(xc * xc, axis=-1, keepdims=True)
    v_b = jax.nn.silu(xc * lax.rsqrt(var + LN_EPS) * lng_ref[...] + lnb_ref[...])
    y_b = _dot(v_b.astype(_BF16), wob_ref[...]) + bob_ref[...]

    g_a = jax.nn.sigmoid(_dot(xn, win_ref[:, 5 * d:6 * d]) + gateb_ref[:, 0:d])
    g_b = jax.nn.sigmoid(_dot(xn, win_ref[:, 6 * d:7 * d]) + gateb_ref[:, d:2 * d])
    mix = (g_a * y_a + g_b * y_b).astype(_BF16)
    x1 = x + _dot(mix, wo_ref[...])
    x1_ref[...] = x1.reshape(nb, tt, d)


def _const_spec(shape):
    zeros = (0,) * len(shape)
    return pl.BlockSpec(shape, lambda *_: zeros, pipeline_mode=pl.Buffered(1))


def _mixer(x, caches, weights, *, nb, tt):
    b, s, d = x.shape
    assert b % nb == 0 and s % tt == 0
    nt = s // tt
    has_cache = caches is not None
    assert (nt == 1) if has_cache else (nb == 1 and tt >= K_B - 1)
    in_specs = [pl.BlockSpec((nb, tt, d), lambda i, j: (i, j, 0))]
    args = [x]
    if has_cache:
        in_specs += [pl.BlockSpec((1, nb, K_A - 1, d), lambda i, j: (0, i, 0, 0)),
                     pl.BlockSpec((1, nb, K_B - 1, d), lambda i, j: (0, i, 0, 0))]
        args += list(caches)
    in_specs += [_const_spec(w.shape) for w in weights]
    args += list(weights)
    out_shape = (jax.ShapeDtypeStruct((b, s, d), _F32),
                 jax.ShapeDtypeStruct((1, b, K_A - 1, d), _F32),
                 jax.ShapeDtypeStruct((1, b, K_B - 1, d), _F32))
    out_specs = (pl.BlockSpec((nb, tt, d), lambda i, j: (i, j, 0)),
                 pl.BlockSpec((1, nb, K_A - 1, d), lambda i, j: (0, i, 0, 0)),
                 pl.BlockSpec((1, nb, K_B - 1, d), lambda i, j: (0, i, 0, 0)))
    return pl.pallas_call(
        functools.partial(_mixer_kernel, nb=nb, tt=tt, nt=nt, has_cache=has_cache),
        grid=(b // nb, nt),
        in_specs=in_specs,
        out_specs=out_specs,
        out_shape=out_shape,
        scratch_shapes=[pltpu.VMEM((nb, HIST_A + tt, d), _F32),
                        pltpu.VMEM((nb * (K_B - 1 + tt) * FRAME_ROWS, LANES), _F32),
                        pltpu.VMEM((nb * tt * FRAME_ROWS, LANES), _F32)],
        compiler_params=pltpu.CompilerParams(
            dimension_semantics=("arbitrary", "arbitrary"),
            vmem_limit_bytes=VMEM_LIMIT_BYTES),
        name="mixer_cache" if has_cache else "mixer_prompt",
    )(*args)


E_PAD = 48
PREFIX_CHUNK = 256
COUNT_SPLIT = 64


def _router_kernel(x1_ref, n2g_ref, wr_ref, rb_ref, pos_ref, wts_ref, offs_ref, *, t):
    assert t % PREFIX_CHUNK == 0 and 2 * t <= COUNT_SPLIT * 256
    xn = _rms_norm(x1_ref[0], n2g_ref[...]).astype(_BF16)
    lg = lax.dot_general(wr_ref[...], xn, (((1,), (1,)), ((), ())),
                         preferred_element_type=_F32) + rb_ref[...]
    row = [lg[r:r + 1, :] for r in range(N_GROUPS + N_EXPERTS)]

    grp = jnp.zeros((1, t), jnp.int32)
    best = row[0]
    for g in range(1, N_GROUPS):
        gt = row[g] > best
        grp = jnp.where(gt, g, grp)
        best = jnp.where(gt, row[g], best)
    denom = None
    for g in range(N_GROUPS):
        e = jnp.exp(row[g] - best)
        denom = e if denom is None else denom + e
    p_top = 1.0 / denom

    sel = []
    for j in range(EXPERTS_PER_GROUP):
        v = row[N_GROUPS + j]
        for g in range(1, N_GROUPS):
            v = jnp.where(grp == g, row[N_GROUPS + g * EXPERTS_PER_GROUP + j], v)
        sel.append(v)
    i1 = jnp.zeros((1, t), jnp.int32)
    v1 = sel[0]
    for j in range(1, EXPERTS_PER_GROUP):
        gt = sel[j] > v1
        i1 = jnp.where(gt, j, i1)
        v1 = jnp.where(gt, sel[j], v1)
    i2 = jnp.where(i1 == 0, 1, 0).astype(jnp.int32)
    v2 = jnp.where(i1 == 0, sel[1], sel[0])
    for j in range(1, EXPERTS_PER_GROUP):
        gt = (sel[j] > v2) & (i1 != j) & (i2 != j)
        i2 = jnp.where(gt, j, i2)
        v2 = jnp.where(gt, sel[j], v2)
    e2x = jnp.exp(v2 - v1)
    ssum = 1.0 + e2x
    w1 = (1.0 / ssum) * p_top
    w2 = (e2x / ssum) * p_top
    e1 = grp * EXPERTS_PER_GROUP + i1
    e2 = grp * EXPERTS_PER_GROUP + i2

    eidx = lax.broadcasted_iota(jnp.int32, (E_PAD, t), 0)
    oh1 = eidx == e1
    oh2 = eidx == e2
    hot = (oh1 | oh2).astype(_F32)
    s_i = lax.broadcasted_iota(jnp.int32, (PREFIX_CHUNK, PREFIX_CHUNK), 0)
    t_i = lax.broadcasted_iota(jnp.int32, (PREFIX_CHUNK, PREFIX_CHUNK), 1)
    before = (s_i < t_i).astype(_BF16)
    carry = jnp.zeros((E_PAD, 1), _F32)
    prefix = []
    for c in range(t // PREFIX_CHUNK):
        hc = hot[:, c * PREFIX_CHUNK:(c + 1) * PREFIX_CHUNK]
        prefix.append(_dot(hc.astype(_BF16), before) + carry)
        carry = carry + jnp.sum(hc, axis=1, keepdims=True)
    prefix = jnp.concatenate(prefix, axis=1)
    counts = jnp.broadcast_to(carry, (E_PAD, LANES))
    c_hi = jnp.floor(counts * (1.0 / COUNT_SPLIT))
    c_lo = counts - c_hi * COUNT_SPLIT
    a_i = lax.broadcasted_iota(jnp.int32, (E_PAD, E_PAD), 0)
    b_i = lax.broadcasted_iota(jnp.int32, (E_PAD, E_PAD), 1)
    lower = (b_i < a_i).astype(_BF16)
    offs = _dot(lower, c_hi.astype(_BF16)) * COUNT_SPLIT + _dot(lower, c_lo.astype(_BF16))
    offs_ref[0] = offs[:, 0:1].astype(jnp.int32)
    slot = offs[:, 0:1] + prefix
    pos1 = jnp.sum(jnp.where(oh1, slot, 0.0), axis=0, keepdims=True)
    pos2 = jnp.sum(jnp.where(oh2, slot, 0.0), axis=0, keepdims=True)
    pos_ref[0, :, 0:t] = (pos1 * FRAME_ROWS).astype(jnp.int32)
    pos_ref[0, :, t:2 * t] = (pos2 * FRAME_ROWS).astype(jnp.int32)
    wts_ref[0, :, 0:t] = w1
    wts_ref[0, :, t:2 * t] = w2


def _router(x1, n2g, wr, rb):
    nt, t, d = x1.shape
    pos, wts, offs = pl.pallas_call(
        functools.partial(_router_kernel, t=t),
        grid=(nt,),
        in_specs=[pl.BlockSpec((1, t, d), lambda k: (k, 0, 0)),
                  _const_spec(n2g.shape), _const_spec(wr.shape), _const_spec(rb.shape)],
        out_specs=(pl.BlockSpec((1, 1, 2 * t), lambda k: (k, 0, 0)),
                   pl.BlockSpec((1, 1, 2 * t), lambda k: (k, 0, 0)),
                   pl.BlockSpec((1, E_PAD, 1), lambda k: (k, 0, 0))),
        out_shape=(jax.ShapeDtypeStruct((nt, 1, 2 * t), jnp.int32),
                   jax.ShapeDtypeStruct((nt, 1, 2 * t), _F32),
                   jax.ShapeDtypeStruct((nt, E_PAD, 1), jnp.int32)),
        compiler_params=pltpu.CompilerParams(
            dimension_semantics=("arbitrary",), vmem_limit_bytes=VMEM_LIMIT_BYTES),
        name=f"router_t{t}",
    )(x1, n2g, wr, rb)
    return pos, wts, offs.reshape(nt, E_PAD)


MOE_SUB = 256
TOKENS_PER_STEP = 8
EXPERT_BLOCK = 256


def _run_copies(src_ref, src_row, dst_ref, dst_row, n, sem, *, max_rows, wait):
    for bit in reversed(range(max_rows.bit_length())):
        size = 1 << bit
        done = lax.shift_left(lax.shift_right_logical(n, bit + 1), bit + 1)

        @pl.when(lax.bitwise_and(lax.shift_right_logical(n, bit), 1) == 1)
        def _():
            s0 = pl.multiple_of((src_row + done) * FRAME_ROWS, FRAME_ROWS)
            d0 = pl.multiple_of((dst_row + done) * FRAME_ROWS, FRAME_ROWS)
            cp = pltpu.make_async_copy(src_ref.at[pl.ds(s0, size * FRAME_ROWS), :],
                                       dst_ref.at[pl.ds(d0, size * FRAME_ROWS), :], sem)
            if wait:
                cp.wait()
            else:
                cp.start()


def _scatter_kernel(offs_ref, gbase_ref, padrow_ref, padlen_ref, nused_ref, pos_ref, x1_ref, n2g_ref,
                    xg_ref, xy_ref, stage_ref, zero_ref, sem, *, t, nt, n_blocks):
    k = pl.program_id(0)
    sub = min(MOE_SUB, t)
    blk8 = EXPERT_BLOCK * FRAME_ROWS

    def sub_body(si, carry):
        r0 = pl.multiple_of(si * sub, sub)
        xn = _rms_norm(x1_ref[0, pl.ds(r0, sub), :], n2g_ref[...])
        _to_frame_major(stage_ref, 0, xn)

        def tok_body(ti, c):
            base = r0 + ti * TOKENS_PER_STEP
            for i in range(TOKENS_PER_STEP):
                row = pl.multiple_of((ti * TOKENS_PER_STEP + i) * FRAME_ROWS, FRAME_ROWS)
                v = stage_ref[pl.ds(row, FRAME_ROWS), :]
                p1 = pl.multiple_of(pos_ref[0, 0, base + i], FRAME_ROWS)
                p2 = pl.multiple_of(pos_ref[0, 0, t + base + i], FRAME_ROWS)
                xy_ref[pl.ds(p1, FRAME_ROWS), :] = v
                xy_ref[pl.ds(p2, FRAME_ROWS), :] = v
            return c

        lax.fori_loop(0, sub // TOKENS_PER_STEP, tok_body, 0)
        return carry

    lax.fori_loop(0, t // sub, sub_body, 0)

    @pl.when(k == nt - 1)
    def _():
        zero_ref[...] = jnp.zeros(zero_ref.shape, _F32)

    def copies(wait):
        def body(e, c):
            start = offs_ref[k, e]
            _run_copies(xy_ref, start, xg_ref, gbase_ref[k, e], offs_ref[k, e + 1] - start, sem,
                        max_rows=t, wait=wait)

            @pl.when(k == nt - 1)
            def _():
                _run_copies(zero_ref, 0, xg_ref, padrow_ref[e], padlen_ref[e], sem,
                            max_rows=EXPERT_BLOCK - 1, wait=wait)
            return c

        lax.fori_loop(0, N_EXPERTS, body, 0)

        @pl.when(k == nt - 1)
        def _():
            def tail_body(b, c):
                cp = pltpu.make_async_copy(
                    zero_ref, xg_ref.at[pl.ds(pl.multiple_of(b * blk8, blk8), blk8), :], sem)
                if wait:
                    cp.wait()
                else:
                    cp.start()
                return c

            lax.fori_loop(nused_ref[0], n_blocks, tail_body, 0)

    copies(wait=False)
    copies(wait=True)


def _scatter(x1, pos, n2g, offs, gbase, padrow, padlen, nused, *, n_blocks):
    nt, t, d = x1.shape
    grid_spec = pltpu.PrefetchScalarGridSpec(
        num_scalar_prefetch=5,
        grid=(nt,),
        in_specs=[
            pl.BlockSpec((1, 1, 2 * t), lambda k, *_: (k, 0, 0), memory_space=pltpu.SMEM),
            pl.BlockSpec((1, t, d), lambda k, *_: (k, 0, 0)),
            _const_spec(n2g.shape),
        ],
        out_specs=pl.BlockSpec(memory_space=pl.ANY),
        scratch_shapes=[pltpu.VMEM((2 * t * FRAME_ROWS, LANES), _F32),
                        pltpu.VMEM((min(MOE_SUB, t) * FRAME_ROWS, LANES), _F32),
                        pltpu.VMEM((EXPERT_BLOCK * FRAME_ROWS, LANES), _F32),
                        pltpu.SemaphoreType.DMA(())],
    )
    return pl.pallas_call(
        functools.partial(_scatter_kernel, t=t, nt=nt, n_blocks=n_blocks),
        grid_spec=grid_spec,
        out_shape=jax.ShapeDtypeStruct((n_blocks * EXPERT_BLOCK * FRAME_ROWS, LANES), _F32),
        compiler_params=pltpu.CompilerParams(
            dimension_semantics=("arbitrary",), vmem_limit_bytes=VMEM_LIMIT_BYTES),
        name=f"moe_scatter_t{t}",
    )(offs, gbase, padrow, padlen, nused, pos, x1, n2g)


def _experts_kernel(bexp_ref, nused_ref, x_ref, wg_ref, wu_ref, wd_ref, y_ref):
    b = pl.program_id(0)

    @pl.when(b < nused_ref[0])
    def _():
        xb = _from_frame_major(x_ref, 0, EXPERT_BLOCK).astype(_BF16)
        h = jax.nn.silu(_dot(xb, wg_ref[0])) * _dot(xb, wu_ref[0])
        _to_frame_major(y_ref, 0, _dot(h.astype(_BF16), wd_ref[0]))

    @pl.when(b >= nused_ref[0])
    def _():
        y_ref[...] = jnp.zeros(y_ref.shape, _F32)


def _experts(xg, bexp, nused, wg, wu, wd):
    rows8 = xg.shape[0]
    blk8 = EXPERT_BLOCK * FRAME_ROWS
    nb = rows8 // blk8
    d = D_MODEL
    grid_spec = pltpu.PrefetchScalarGridSpec(
        num_scalar_prefetch=2,
        grid=(nb,),
        in_specs=[
            pl.BlockSpec((blk8, LANES), lambda b, be, nu: (jnp.minimum(b, nu[0] - 1), 0)),
            pl.BlockSpec((1, d, D_EXPERT), lambda b, be, nu: (be[b], 0, 0)),
            pl.BlockSpec((1, d, D_EXPERT), lambda b, be, nu: (be[b], 0, 0)),
            pl.BlockSpec((1, D_EXPERT, d), lambda b, be, nu: (be[b], 0, 0)),
        ],
        out_specs=pl.BlockSpec((blk8, LANES), lambda b, be, nu: (b, 0)),
    )
    return pl.pallas_call(
        _experts_kernel,
        grid_spec=grid_spec,
        out_shape=jax.ShapeDtypeStruct(xg.shape, _F32),
        compiler_params=pltpu.CompilerParams(
            dimension_semantics=("arbitrary",), vmem_limit_bytes=VMEM_LIMIT_BYTES),
        name="moe_experts",
    )(bexp, nused, xg, wg, wu, wd)


def _combine_kernel(offs_ref, gbase_ref, pos_ref, wts_ref, x1_ref, fg_ref, yg_ref, out_ref,
                    yl_ref, stage_ref, sem, *, t):
    k = pl.program_id(0)
    sub = min(MOE_SUB, t)

    def copies(wait):
        def body(e, c):
            start = offs_ref[k, e]
            _run_copies(yg_ref, gbase_ref[k, e], yl_ref, start, offs_ref[k, e + 1] - start, sem,
                        max_rows=t, wait=wait)
            return c

        lax.fori_loop(0, N_EXPERTS, body, 0)

    copies(wait=False)
    copies(wait=True)

    def sub_body(si, carry):
        r0 = pl.multiple_of(si * sub, sub)

        def tok_body(ti, c):
            base = r0 + ti * TOKENS_PER_STEP
            for i in range(TOKENS_PER_STEP):
                p1 = pl.multiple_of(pos_ref[0, 0, base + i], FRAME_ROWS)
                p2 = pl.multiple_of(pos_ref[0, 0, t + base + i], FRAME_ROWS)
                v = (wts_ref[0, 0, base + i] * yl_ref[pl.ds(p1, FRAME_ROWS), :]
                     + wts_ref[0, 0, t + base + i] * yl_ref[pl.ds(p2, FRAME_ROWS), :])
                row = pl.multiple_of((ti * TOKENS_PER_STEP + i) * FRAME_ROWS, FRAME_ROWS)
                stage_ref[pl.ds(row, FRAME_ROWS), :] = v
            return c

        lax.fori_loop(0, sub // TOKENS_PER_STEP, tok_body, 0)
        x2 = x1_ref[0, pl.ds(r0, sub), :] + _from_frame_major(stage_ref, 0, sub)
        out_ref[0, pl.ds(r0, sub), :] = _rms_norm(x2, fg_ref[...])
        return carry

    lax.fori_loop(0, t // sub, sub_body, 0)


def _combine(x1, pos, wts, fg, yg, offs, gbase):
    nt, t, d = x1.shape
    grid_spec = pltpu.PrefetchScalarGridSpec(
        num_scalar_prefetch=2,
        grid=(nt,),
        in_specs=[
            pl.BlockSpec((1, 1, 2 * t), lambda k, *_: (k, 0, 0), memory_space=pltpu.SMEM),
            pl.BlockSpec((1, 1, 2 * t), lambda k, *_: (k, 0, 0), memory_space=pltpu.SMEM),
            pl.BlockSpec((1, t, d), lambda k, *_: (k, 0, 0)),
            _const_spec(fg.shape),
            pl.BlockSpec(memory_space=pl.ANY),
        ],
        out_specs=pl.BlockSpec((1, t, d), lambda k, *_: (k, 0, 0)),
        scratch_shapes=[pltpu.VMEM((2 * t * FRAME_ROWS, LANES), _F32),
                        pltpu.VMEM((min(MOE_SUB, t) * FRAME_ROWS, LANES), _F32),
                        pltpu.SemaphoreType.DMA(())],
    )
    return pl.pallas_call(
        functools.partial(_combine_kernel, t=t),
        grid_spec=grid_spec,
        out_shape=jax.ShapeDtypeStruct((nt, t, d), _F32),
        compiler_params=pltpu.CompilerParams(
            dimension_semantics=("arbitrary",), vmem_limit_bytes=VMEM_LIMIT_BYTES),
        name=f"moe_combine_t{t}",
    )(offs, gbase, pos, wts, x1, fg, yg)


def _route_and_moe(x1, router_w, moe_w):
    n2g, fg, wg, wu, wd = moe_w
    nt, t, _ = x1.shape
    pos, wts, offs = _router(x1, *router_w)
    cnt = offs[:, 1:N_EXPERTS + 1] - offs[:, :N_EXPERTS]
    tot = jnp.sum(cnt, axis=0)
    nblk = (tot + (EXPERT_BLOCK - 1)) // EXPERT_BLOCK
    blk_end = jnp.cumsum(nblk)
    gstart = (blk_end - nblk) * EXPERT_BLOCK
    gbase = (gstart[None, :] + jnp.cumsum(cnt, axis=0) - cnt).astype(jnp.int32)
    padrow = (gstart + tot).astype(jnp.int32)
    padlen = (nblk * EXPERT_BLOCK - tot).astype(jnp.int32)
    n_blocks = (2 * nt * t + EXPERT_BLOCK - 1) // EXPERT_BLOCK + N_EXPERTS
    bexp = jnp.minimum(jnp.searchsorted(blk_end, jnp.arange(n_blocks), side="right"),
                       N_EXPERTS - 1).astype(jnp.int32)
    nused = blk_end[N_EXPERTS - 1:].astype(jnp.int32)
    xg = _scatter(x1, pos, n2g, offs, gbase, padrow, padlen, nused, n_blocks=n_blocks)
    yg = _experts(xg, bexp, nused, wg, wu, wd)
    return _combine(x1, pos, wts, fg, yg, offs, gbase)


def kernel(x_prompt, x_sample, cache_conv_a, cache_conv_b, norm1_g, w_in, gate_b, conv_a_w, w_out_a,
           conv_b_w, conv_b_b, ln_b_g, ln_b_b, w_out_b, b_out_b, w_o, norm2_g, router_group_w,
           router_group_b, router_expert_w, router_expert_b, exp_w_gate, exp_w_up, exp_w_down,
           final_norm_g):
    d = D_MODEL
    mixer_w = (norm1_g, w_in[0].astype(_BF16), gate_b, conv_a_w[0], w_out_a[0].astype(_BF16),
               conv_b_w[0].reshape(K_B, FRAME_ROWS, LANES), conv_b_b.reshape(FRAME_ROWS, LANES),
               ln_b_g, ln_b_b, w_out_b[0].astype(_BF16), b_out_b, w_o[0].astype(_BF16))
    n_router = N_GROUPS + N_EXPERTS
    wr = jnp.concatenate([router_group_w[0].T, router_expert_w[0].T,
                          jnp.zeros((E_PAD - n_router, d), _F32)], axis=0).astype(_BF16)
    rb = jnp.concatenate([router_group_b[0], router_expert_b[0],
                          jnp.zeros((E_PAD - n_router,), _F32)]).reshape(E_PAD, 1)
    router_w = (norm2_g, wr, rb)
    moe_w = (norm2_g, final_norm_g.reshape(1, d), exp_w_gate[0].astype(_BF16),
             exp_w_up[0].astype(_BF16), exp_w_down[0].astype(_BF16))

    x1_p, na_p, nb_p = _mixer(x_prompt, None, mixer_w, nb=1, tt=256)
    x1_s, na_s, nb_s = _mixer(x_sample, (cache_conv_a, cache_conv_b), mixer_w, nb=16, tt=16)
    y_p = _route_and_moe(x1_p, router_w, moe_w)
    bs, ts, _ = x_sample.shape
    y_s = _route_and_moe(x1_s.reshape(1, bs * ts, d), router_w, moe_w)
    return (y_p, y_s.reshape(bs, ts, d), na_p, na_s, nb_p, nb_s)
```

```python
import functools

import jax
import jax.numpy as jnp
from jax import lax
from jax.experimental import pallas as pl
from jax.experimental.pallas import tpu as pltpu

D_MODEL = 1024
K_A = 3
K_B = 31
N_GROUPS = 4
EXPERTS_PER_GROUP = 8
N_EXPERTS = N_GROUPS * EXPERTS_PER_GROUP
D_EXPERT = D_MODEL // 2
RMS_EPS = 1e-6
LN_EPS = 1e-5

SUBLANES = 8
LANES = 128
FRAME_ROWS = D_MODEL // LANES
assert FRAME_ROWS == SUBLANES
HIST_A = SUBLANES
assert HIST_A >= K_A - 1

VMEM_LIMIT_BYTES = 58 * 1024 * 1024

_F32 = jnp.float32
_BF16 = jnp.bfloat16


def _rms_norm(x, g):
    return x * lax.rsqrt(jnp.mean(x * x, axis=-1, keepdims=True) + RMS_EPS) * g


def _dot(a, b):
    return jnp.dot(a, b, preferred_element_type=_F32)


def _to_frame_major(dst_ref, first_frame, x):
    n = x.shape[0]
    for j in range(FRAME_ROWS):
        dst_ref[pl.ds(first_frame * FRAME_ROWS + j, n, stride=FRAME_ROWS), :] = (
            x[:, j * LANES:(j + 1) * LANES])


def _from_frame_major(src_ref, first_frame, n):
    return jnp.concatenate(
        [src_ref[pl.ds(first_frame * FRAME_ROWS + j, n, stride=FRAME_ROWS), :]
         for j in range(FRAME_ROWS)], axis=-1)


def _mixer_kernel(*refs, nb, tt, nt, has_cache):
    if has_cache:
        x_ref, ca_ref, cb_ref = refs[:3]
        refs = refs[3:]
    else:
        x_ref = refs[0]
        refs = refs[1:]
    (n1g_ref, win_ref, gateb_ref, caw_ref, woa_ref, cbw_ref, cbb_ref, lng_ref, lnb_ref,
     wob_ref, bob_ref, wo_ref, x1_ref, newa_ref, newb_ref, sa_ref, sb_ref, cv_ref) = refs
    d = D_MODEL
    m = nb * tt
    hb = K_B - 1
    seq_frames = hb + tt
    t = pl.program_id(1)

    @pl.when(t == 0)
    def _():
        if has_cache:
            sa_ref[:, HIST_A - (K_A - 1):HIST_A, :] = ca_ref[0]
            for s in range(nb):
                _to_frame_major(sb_ref, s * seq_frames, cb_ref[0, s])
        else:
            sa_ref[:, 0:HIST_A, :] = jnp.zeros((nb, HIST_A, d), _F32)
            for s in range(nb):
                sb_ref[pl.ds(s * seq_frames * FRAME_ROWS, hb * FRAME_ROWS), :] = (
                    jnp.zeros((hb * FRAME_ROWS, LANES), _F32))

    x = x_ref[...].reshape(m, d)
    xn = _rms_norm(x, n1g_ref[...]).astype(_BF16)

    z = _dot(xn, win_ref[:, d:2 * d]) * _dot(xn, win_ref[:, 2 * d:3 * d])
    sa_ref[:, HIST_A:HIST_A + tt, :] = z.reshape(nb, tt, d)
    conv_a = None
    for k in range(K_A):
        lo = HIST_A - (K_A - 1) + k
        term = sa_ref[:, lo:lo + tt, :] * caw_ref[k:k + 1, :]
        conv_a = term if conv_a is None else conv_a + term
    newa_ref[0] = sa_ref[:, HIST_A + tt - (K_A - 1):HIST_A + tt, :]
    if nt > 1:
        sa_ref[:, HIST_A - (K_A - 1):HIST_A, :] = sa_ref[:, HIST_A + tt - (K_A - 1):HIST_A + tt, :]
    b_a = _dot(xn, win_ref[:, 0:d])
    y_a = _dot((b_a * conv_a.reshape(m, d)).astype(_BF16), woa_ref[...])

    u = _dot(xn, win_ref[:, 3 * d:4 * d]) * jax.nn.sigmoid(_dot(xn, win_ref[:, 4 * d:5 * d]))
    u3 = u.reshape(nb, tt, d)
    if tt >= hb:
        newb_ref[0] = u3[:, tt - hb:tt, :]
    else:
        newb_ref[0, :, 0:hb - tt, :] = cb_ref[0, :, tt:hb, :]
        newb_ref[0, :, hb - tt:hb, :] = u3
    for s in range(nb):
        _to_frame_major(sb_ref, s * seq_frames + hb, u[s * tt:(s + 1) * tt, :])
    for s in range(nb):
        acc = None
        for k in range(K_B):
            win = sb_ref[pl.ds((s * seq_frames + k) * FRAME_ROWS, tt * FRAME_ROWS), :]
            term = win.reshape(tt, FRAME_ROWS, LANES) * cbw_ref[k]
            acc = term if acc is None else acc + term
        acc = acc + cbb_ref[...]
        cv_ref[pl.ds(s * tt * FRAME_ROWS, tt * FRAME_ROWS), :] = acc.reshape(tt * FRAME_ROWS, LANES)
    if nt > 1:
        sb_ref[pl.ds(0, hb * FRAME_ROWS), :] = sb_ref[pl.ds(tt * FRAME_ROWS, hb * FRAME_ROWS), :]
    cb = _from_frame_major(cv_ref, 0, m)
    mu = jnp.mean(cb, axis=-1, keepdims=True)
    xc = cb - mu
    var = jnp.mean(xc * xc, axis=-1, keepdims=True)
    v_b = jax.nn.silu(xc * lax.rsqrt(var + LN_EPS) * lng_ref[...] + lnb_ref[...])
    y_b = _dot(v_b.astype(_BF16), wob_ref[...]) + bob_ref[...]

    g_a = jax.nn.sigmoid(_dot(xn, win_ref[:, 5 * d:6 * d]) + gateb_ref[:, 0:d])
    g_b = jax.nn.sigmoid(_dot(xn, win_ref[:, 6 * d:7 * d]) + gateb_ref[:, d:2 * d])
    mix = (g_a * y_a + g_b * y_b).astype(_BF16)
    x1 = x + _dot(mix, wo_ref[...])
    x1_ref[...] = x1.reshape(nb, tt, d)


def _const_spec(shape):
    zeros = (0,) * len(shape)
    return pl.BlockSpec(shape, lambda *_: zeros, pipeline_mode=pl.Buffered(1))


def _mixer(x, caches, weights, *, nb, tt):
    b, s, d = x.shape
    assert b % nb == 0 and s % tt == 0
    nt = s // tt
    has_cache = caches is not None
    assert (nt == 1) if has_cache else (nb == 1 and tt >= K_B - 1)
    in_specs = [pl.BlockSpec((nb, tt, d), lambda i, j: (i, j, 0))]
    args = [x]
    if has_cache:
        in_specs += [pl.BlockSpec((1, nb, K_A - 1, d), lambda i, j: (0, i, 0, 0)),
                     pl.BlockSpec((1, nb, K_B - 1, d), lambda i, j: (0, i, 0, 0))]
        args += list(caches)
    in_specs += [_const_spec(w.shape) for w in weights]
    args += list(weights)
    out_shape = (jax.ShapeDtypeStruct((b, s, d), _F32),
                 jax.ShapeDtypeStruct((1, b, K_A - 1, d), _F32),
                 jax.ShapeDtypeStruct((1, b, K_B - 1, d), _F32))
    out_specs = (pl.BlockSpec((nb, tt, d), lambda i, j: (i, j, 0)),
                 pl.BlockSpec((1, nb, K_A - 1, d), lambda i, j: (0, i, 0, 0)),
                 pl.BlockSpec((1, nb, K_B - 1, d), lambda i, j: (0, i, 0, 0)))
    return pl.pallas_call(
        functools.partial(_mixer_kernel, nb=nb, tt=tt, nt=nt, has_cache=has_cache),
        grid=(b // nb, nt),
        in_specs=in_specs,
        out_specs=out_specs,
        out_shape=out_shape,
        scratch_shapes=[pltpu.VMEM((nb, HIST_A + tt, d), _F32),
                        pltpu.VMEM((nb * (K_B - 1 + tt) * FRAME_ROWS, LANES), _F32),
                        pltpu.VMEM((nb * tt * FRAME_ROWS, LANES), _F32)],
        compiler_params=pltpu.CompilerParams(
            dimension_semantics=("arbitrary", "arbitrary"),
            vmem_limit_bytes=VMEM_LIMIT_BYTES),
        name="mixer_cache" if has_cache else "mixer_prompt",
    )(*args)


E_PAD = 48
PREFIX_CHUNK = 256
COUNT_SPLIT = 64


def _router_kernel(x1_ref, n2g_ref, wr_ref, rb_ref, pos_ref, wts_ref, offs_ref, *, t):
    assert t % PREFIX_CHUNK == 0 and 2 * t <= COUNT_SPLIT * 256
    xn = _rms_norm(x1_ref[0], n2g_ref[...]).astype(_BF16)
    lg = lax.dot_general(wr_ref[...], xn, (((1,), (1,)), ((), ())),
                         preferred_element_type=_F32) + rb_ref[...]
    row = [lg[r:r + 1, :] for r in range(N_GROUPS + N_EXPERTS)]

    grp = jnp.zeros((1, t), jnp.int32)
    best = row[0]
    for g in range(1, N_GROUPS):
        gt = row[g] > best
        grp = jnp.where(gt, g, grp)
        best = jnp.where(gt, row[g], best)
    denom = None
    for g in range(N_GROUPS):
        e = jnp.exp(row[g] - best)
        denom = e if denom is None else denom + e
    p_top = 1.0 / denom

    sel = []
    for j in range(EXPERTS_PER_GROUP):
        v = row[N_GROUPS + j]
        for g in range(1, N_GROUPS):
            v = jnp.where(grp == g, row[N_GROUPS + g * EXPERTS_PER_GROUP + j], v)
        sel.append(v)
    i1 = jnp.zeros((1, t), jnp.int32)
    v1 = sel[0]
    for j in range(1, EXPERTS_PER_GROUP):
        gt = sel[j] > v1
        i1 = jnp.where(gt, j, i1)
        v1 = jnp.where(gt, sel[j], v1)
    i2 = jnp.where(i1 == 0, 1, 0).astype(jnp.int32)
    v2 = jnp.where(i1 == 0, sel[1], sel[0])
    for j in range(1, EXPERTS_PER_GROUP):
        gt = (sel[j] > v2) & (i1 != j) & (i2 != j)
        i2 = jnp.where(gt, j, i2)
        v2 = jnp.where(gt, sel[j], v2)
    e2x = jnp.exp(v2 - v1)
    ssum = 1.0 + e2x
    w1 = (1.0 / ssum) * p_top
    w2 = (e2x / ssum) * p_top
    e1 = grp * EXPERTS_PER_GROUP + i1
    e2 = grp * EXPERTS_PER_GROUP + i2

    eidx = lax.broadcasted_iota(jnp.int32, (E_PAD, t), 0)
    oh1 = eidx == e1
    oh2 = eidx == e2
    hot = (oh1 | oh2).astype(_F32)
    s_i = lax.broadcasted_iota(jnp.int32, (PREFIX_CHUNK, PREFIX_CHUNK), 0)
    t_i = lax.broadcasted_iota(jnp.int32, (PREFIX_CHUNK, PREFIX_CHUNK), 1)
    before = (s_i < t_i).astype(_BF16)
    carry = jnp.zeros((E_PAD, 1), _F32)
    prefix = []
    for c in range(t // PREFIX_CHUNK):
        hc = hot[:, c * PREFIX_CHUNK:(c + 1) * PREFIX_CHUNK]
        prefix.append(_dot(hc.astype(_BF16), before) + carry)
        carry = carry + jnp.sum(hc, axis=1, keepdims=True)
    prefix = jnp.concatenate(prefix, axis=1)
    counts = jnp.broadcast_to(carry, (E_PAD, LANES))
    c_hi = jnp.floor(counts * (1.0 / COUNT_SPLIT))
    c_lo = counts - c_hi * COUNT_SPLIT
    a_i = lax.broadcasted_iota(jnp.int32, (E_PAD, E_PAD), 0)
    b_i = lax.broadcasted_iota(jnp.int32, (E_PAD, E_PAD), 1)
    lower = (b_i < a_i).astype(_BF16)
    offs = _dot(lower, c_hi.astype(_BF16)) * COUNT_SPLIT + _dot(lower, c_lo.astype(_BF16))
    offs_ref[0] = offs[:, 0:1].astype(jnp.int32)
    slot = offs[:, 0:1] + prefix
    pos1 = jnp.sum(jnp.where(oh1, slot, 0.0), axis=0, keepdims=True)
    pos2 = jnp.sum(jnp.where(oh2, slot, 0.0), axis=0, keepdims=True)
    pos_ref[0, :, 0:t] = (pos1 * FRAME_ROWS).astype(jnp.int32)
    pos_ref[0, :, t:2 * t] = (pos2 * FRAME_ROWS).astype(jnp.int32)
    wts_ref[0, :, 0:t] = w1
    wts_ref[0, :, t:2 * t] = w2


def _router(x1, n2g, wr, rb):
    nt, t, d = x1.shape
    pos, wts, offs = pl.pallas_call(
        functools.partial(_router_kernel, t=t),
        grid=(nt,),
        in_specs=[pl.BlockSpec((1, t, d), lambda k: (k, 0, 0)),
                  _const_spec(n2g.shape), _const_spec(wr.shape), _const_spec(rb.shape)],
        out_specs=(pl.BlockSpec((1, 1, 2 * t), lambda k: (k, 0, 0)),
                   pl.BlockSpec((1, 1, 2 * t), lambda k: (k, 0, 0)),
                   pl.BlockSpec((1, E_PAD, 1), lambda k: (k, 0, 0))),
        out_shape=(jax.ShapeDtypeStruct((nt, 1, 2 * t), jnp.int32),
                   jax.ShapeDtypeStruct((nt, 1, 2 * t), _F32),
                   jax.ShapeDtypeStruct((nt, E_PAD, 1), jnp.int32)),
        compiler_params=pltpu.CompilerParams(
            dimension_semantics=("arbitrary",), vmem_limit_bytes=VMEM_LIMIT_BYTES),
        name=f"router_t{t}",
    )(x1, n2g, wr, rb)
    return pos, wts, offs.reshape(nt, E_PAD)


MOE_SUB = 256
TOKENS_PER_STEP = 8
EXPERT_BLOCK = 256


def _run_copies(src_ref, src_row, dst_ref, dst_row, n, sem, *, max_rows, wait):
    for bit in reversed(range(max_rows.bit_length())):
        size = 1 << bit
        done = lax.shift_left(lax.shift_right_logical(n, bit + 1), bit + 1)

        @pl.when(lax.bitwise_and(lax.shift_right_logical(n, bit), 1) == 1)
        def _():
            s0 = pl.multiple_of((src_row + done) * FRAME_ROWS, FRAME_ROWS)
            d0 = pl.multiple_of((dst_row + done) * FRAME_ROWS, FRAME_ROWS)
            cp = pltpu.make_async_copy(src_ref.at[pl.ds(s0, size * FRAME_ROWS), :],
                                       dst_ref.at[pl.ds(d0, size * FRAME_ROWS), :], sem)
            if wait:
                cp.wait()
            else:
                cp.start()


def _scatter_kernel(offs_ref, gbase_ref, padrow_ref, padlen_ref, nused_ref, pos_ref, x1_ref, n2g_ref,
                    *refs, t, nt, n_blocks, fill, chained):
    xg_ref, xy_ref, stage_ref, zero_ref, sem, fill_sem = refs[1:] if chained else refs
    k = pl.program_id(0)
    sub = min(MOE_SUB, t)
    blk8 = EXPERT_BLOCK * FRAME_ROWS

    def sub_body(si, carry):
        r0 = pl.multiple_of(si * sub, sub)
        xn = _rms_norm(x1_ref[0, pl.ds(r0, sub), :], n2g_ref[...])
        _to_frame_major(stage_ref, 0, xn)

        def tok_body(ti, c):
            base = r0 + ti * TOKENS_PER_STEP
            for i in range(TOKENS_PER_STEP):
                row = pl.multiple_of((ti * TOKENS_PER_STEP + i) * FRAME_ROWS, FRAME_ROWS)
                v = stage_ref[pl.ds(row, FRAME_ROWS), :]
                p1 = pl.multiple_of(pos_ref[0, 0, base + i], FRAME_ROWS)
                p2 = pl.multiple_of(pos_ref[0, 0, t + base + i], FRAME_ROWS)
                xy_ref[pl.ds(p1, FRAME_ROWS), :] = v
                xy_ref[pl.ds(p2, FRAME_ROWS), :] = v
            return c

        lax.fori_loop(0, sub // TOKENS_PER_STEP, tok_body, 0)
        return carry

    lax.fori_loop(0, t // sub, sub_body, 0)

    def run_body(e, c):
        start = offs_ref[k, e]
        _run_copies(xy_ref, start, xg_ref, gbase_ref[k, e], offs_ref[k, e + 1] - start, sem,
                    max_rows=t, wait=False)
        return c

    lax.fori_loop(0, N_EXPERTS, run_body, 0)

    def fills(wait):
        def pad_body(e, c):
            _run_copies(zero_ref, 0, xg_ref, padrow_ref[e], padlen_ref[e], fill_sem,
                        max_rows=EXPERT_BLOCK - 1, wait=wait)
            return c

        def tail_body(b, c):
            cp = pltpu.make_async_copy(
                zero_ref, xg_ref.at[pl.ds(pl.multiple_of(b * blk8, blk8), blk8), :], fill_sem)
            if wait:
                cp.wait()
            else:
                cp.start()
            return c

        lax.fori_loop(0, N_EXPERTS, pad_body, 0)
        lax.fori_loop(nused_ref[0], n_blocks, tail_body, 0)

    if fill:
        @pl.when(k == nt - 1)
        def _():
            zero_ref[...] = jnp.zeros(zero_ref.shape, _F32)
            fills(wait=False)
            fills(wait=True)

    pltpu.make_async_copy(xy_ref, xg_ref.at[pl.ds(0, 2 * t * FRAME_ROWS), :], sem).wait()


def _scatter(x1, pos, n2g, offs, gbase, padrow, padlen, nused, xg_prev, *, n_blocks, fill):
    nt, t, d = x1.shape
    chained = xg_prev is not None
    n_prefetch = 5
    in_specs = [
        pl.BlockSpec((1, 1, 2 * t), lambda k, *_: (k, 0, 0), memory_space=pltpu.SMEM),
        pl.BlockSpec((1, t, d), lambda k, *_: (k, 0, 0)),
        _const_spec(n2g.shape),
    ]
    args = [offs, gbase, padrow, padlen, nused, pos, x1, n2g]
    aliases = {}
    if chained:
        in_specs.append(pl.BlockSpec(memory_space=pl.ANY))
        aliases = {len(args): 0}
        args.append(xg_prev)
    grid_spec = pltpu.PrefetchScalarGridSpec(
        num_scalar_prefetch=n_prefetch,
        grid=(nt,),
        in_specs=in_specs,
        out_specs=pl.BlockSpec(memory_space=pl.ANY),
        scratch_shapes=[pltpu.VMEM((2 * t * FRAME_ROWS, LANES), _F32),
                        pltpu.VMEM((min(MOE_SUB, t) * FRAME_ROWS, LANES), _F32),
                        pltpu.VMEM((EXPERT_BLOCK * FRAME_ROWS, LANES), _F32),
                        pltpu.SemaphoreType.DMA(()),
                        pltpu.SemaphoreType.DMA(())],
    )
    return pl.pallas_call(
        functools.partial(_scatter_kernel, t=t, nt=nt, n_blocks=n_blocks, fill=fill, chained=chained),
        grid_spec=grid_spec,
        out_shape=jax.ShapeDtypeStruct((n_blocks * EXPERT_BLOCK * FRAME_ROWS, LANES), _F32),
        input_output_aliases=aliases,
        compiler_params=pltpu.CompilerParams(
            dimension_semantics=("arbitrary",), vmem_limit_bytes=VMEM_LIMIT_BYTES),
        name=f"moe_scatter_t{t}",
    )(*args)


def _experts_kernel(bexp_ref, nused_ref, x_ref, wg_ref, wu_ref, wd_ref, y_ref):
    b = pl.program_id(0)

    @pl.when(b < nused_ref[0])
    def _():
        xb = _from_frame_major(x_ref, 0, EXPERT_BLOCK).astype(_BF16)
        h = jax.nn.silu(_dot(xb, wg_ref[0])) * _dot(xb, wu_ref[0])
        _to_frame_major(y_ref, 0, _dot(h.astype(_BF16), wd_ref[0]))

    @pl.when(b >= nused_ref[0])
    def _():
        y_ref[...] = jnp.zeros(y_ref.shape, _F32)


def _experts(xg, bexp, nused, wg, wu, wd):
    rows8 = xg.shape[0]
    blk8 = EXPERT_BLOCK * FRAME_ROWS
    nb = rows8 // blk8
    d = D_MODEL
    grid_spec = pltpu.PrefetchScalarGridSpec(
        num_scalar_prefetch=2,
        grid=(nb,),
        in_specs=[
            pl.BlockSpec((blk8, LANES), lambda b, be, nu: (jnp.minimum(b, nu[0] - 1), 0)),
            pl.BlockSpec((1, d, D_EXPERT), lambda b, be, nu: (be[b], 0, 0)),
            pl.BlockSpec((1, d, D_EXPERT), lambda b, be, nu: (be[b], 0, 0)),
            pl.BlockSpec((1, D_EXPERT, d), lambda b, be, nu: (be[b], 0, 0)),
        ],
        out_specs=pl.BlockSpec((blk8, LANES), lambda b, be, nu: (b, 0)),
    )
    return pl.pallas_call(
        _experts_kernel,
        grid_spec=grid_spec,
        out_shape=jax.ShapeDtypeStruct(xg.shape, _F32),
        compiler_params=pltpu.CompilerParams(
            dimension_semantics=("arbitrary",), vmem_limit_bytes=VMEM_LIMIT_BYTES),
        name="moe_experts",
    )(bexp, nused, xg, wg, wu, wd)


def _combine_kernel(offs_ref, gbase_ref, pos_ref, wts_ref, x1_ref, fg_ref, yg_ref, out_ref,
                    yl_ref, stage_ref, sem, *, t):
    k = pl.program_id(0)
    sub = min(MOE_SUB, t)

    def run_body(e, c):
        start = offs_ref[k, e]
        _run_copies(yg_ref, gbase_ref[k, e], yl_ref, start, offs_ref[k, e + 1] - start, sem,
                    max_rows=t, wait=False)
        return c

    lax.fori_loop(0, N_EXPERTS, run_body, 0)
    pltpu.make_async_copy(yg_ref.at[pl.ds(0, 2 * t * FRAME_ROWS), :], yl_ref, sem).wait()

    def sub_body(si, carry):
        r0 = pl.multiple_of(si * sub, sub)

        def tok_body(ti, c):
            base = r0 + ti * TOKENS_PER_STEP
            for i in range(TOKENS_PER_STEP):
                p1 = pl.multiple_of(pos_ref[0, 0, base + i], FRAME_ROWS)
                p2 = pl.multiple_of(pos_ref[0, 0, t + base + i], FRAME_ROWS)
                v = (wts_ref[0, 0, base + i] * yl_ref[pl.ds(p1, FRAME_ROWS), :]
                     + wts_ref[0, 0, t + base + i] * yl_ref[pl.ds(p2, FRAME_ROWS), :])
                row = pl.multiple_of((ti * TOKENS_PER_STEP + i) * FRAME_ROWS, FRAME_ROWS)
                stage_ref[pl.ds(row, FRAME_ROWS), :] = v
            return c

        lax.fori_loop(0, sub // TOKENS_PER_STEP, tok_body, 0)
        x2 = x1_ref[0, pl.ds(r0, sub), :] + _from_frame_major(stage_ref, 0, sub)
        out_ref[0, pl.ds(r0, sub), :] = _rms_norm(x2, fg_ref[...])
        return carry

    lax.fori_loop(0, t // sub, sub_body, 0)


def _combine(x1, pos, wts, fg, yg, offs, gbase):
    nt, t, d = x1.shape
    grid_spec = pltpu.PrefetchScalarGridSpec(
        num_scalar_prefetch=2,
        grid=(nt,),
        in_specs=[
            pl.BlockSpec((1, 1, 2 * t), lambda k, *_: (k, 0, 0), memory_space=pltpu.SMEM),
            pl.BlockSpec((1, 1, 2 * t), lambda k, *_: (k, 0, 0), memory_space=pltpu.SMEM),
            pl.BlockSpec((1, t, d), lambda k, *_: (k, 0, 0)),
            _const_spec(fg.shape),
            pl.BlockSpec(memory_space=pl.ANY),
        ],
        out_specs=pl.BlockSpec((1, t, d), lambda k, *_: (k, 0, 0)),
        scratch_shapes=[pltpu.VMEM((2 * t * FRAME_ROWS, LANES), _F32),
                        pltpu.VMEM((min(MOE_SUB, t) * FRAME_ROWS, LANES), _F32),
                        pltpu.SemaphoreType.DMA(())],
    )
    return pl.pallas_call(
        functools.partial(_combine_kernel, t=t),
        grid_spec=grid_spec,
        out_shape=jax.ShapeDtypeStruct((nt, t, d), _F32),
        compiler_params=pltpu.CompilerParams(
            dimension_semantics=("arbitrary",), vmem_limit_bytes=VMEM_LIMIT_BYTES),
        name=f"moe_combine_t{t}",
    )(offs, gbase, pos, wts, x1, fg, yg)


def _route_and_moe(x1_groups, router_w, moe_w):
    n2g, fg, wg, wu, wd = moe_w
    routed = [_router(x1, *router_w) for x1 in x1_groups]
    offs_all = jnp.concatenate([offs for _, _, offs in routed], axis=0)
    cnt = offs_all[:, 1:N_EXPERTS + 1] - offs_all[:, :N_EXPERTS]
    tot = jnp.sum(cnt, axis=0)
    nblk = (tot + (EXPERT_BLOCK - 1)) // EXPERT_BLOCK
    blk_end = jnp.cumsum(nblk)
    gstart = (blk_end - nblk) * EXPERT_BLOCK
    gbase_all = (gstart[None, :] + jnp.cumsum(cnt, axis=0) - cnt).astype(jnp.int32)
    padrow = (gstart + tot).astype(jnp.int32)
    padlen = (nblk * EXPERT_BLOCK - tot).astype(jnp.int32)
    n_rows = sum(2 * x1.shape[0] * x1.shape[1] for x1 in x1_groups)
    n_blocks = (n_rows + EXPERT_BLOCK - 1) // EXPERT_BLOCK + N_EXPERTS
    bexp = jnp.minimum(jnp.sum(jnp.arange(n_blocks)[:, None] >= blk_end[None, :], axis=1),
                       N_EXPERTS - 1).astype(jnp.int32)
    nused = blk_end[N_EXPERTS - 1:].astype(jnp.int32)
    gbases = []
    tile0 = 0
    for x1 in x1_groups:
        gbases.append(gbase_all[tile0:tile0 + x1.shape[0]])
        tile0 += x1.shape[0]
    xg = None
    for i, (x1, (pos, _, offs)) in enumerate(zip(x1_groups, routed)):
        xg = _scatter(x1, pos, n2g, offs, gbases[i], padrow, padlen, nused, xg,
                      n_blocks=n_blocks, fill=(i == len(x1_groups) - 1))
    yg = _experts(xg, bexp, nused, wg, wu, wd)
    return [_combine(x1, pos, wts, fg, yg, offs, gbase)
            for x1, (pos, wts, offs), gbase in zip(x1_groups, routed, gbases)]


def kernel(x_prompt, x_sample, cache_conv_a, cache_conv_b, norm1_g, w_in, gate_b, conv_a_w, w_out_a,
           conv_b_w, conv_b_b, ln_b_g, ln_b_b, w_out_b, b_out_b, w_o, norm2_g, router_group_w,
           router_group_b, router_expert_w, router_expert_b, exp_w_gate, exp_w_up, exp_w_down,
           final_norm_g):
    d = D_MODEL
    mixer_w = (norm1_g, w_in[0].astype(_BF16), gate_b, conv_a_w[0], w_out_a[0].astype(_BF16),
               conv_b_w[0].reshape(K_B, FRAME_ROWS, LANES), conv_b_b.reshape(FRAME_ROWS, LANES),
               ln_b_g, ln_b_b, w_out_b[0].astype(_BF16), b_out_b, w_o[0].astype(_BF16))
    n_router = N_GROUPS + N_EXPERTS
    wr = jnp.concatenate([router_group_w[0].T, router_expert_w[0].T,
                          jnp.zeros((E_PAD - n_router, d), _F32)], axis=0).astype(_BF16)
    rb = jnp.concatenate([router_group_b[0], router_expert_b[0],
                          jnp.zeros((E_PAD - n_router,), _F32)]).reshape(E_PAD, 1)
    router_w = (norm2_g, wr, rb)
    moe_w = (norm2_g, final_norm_g.reshape(1, d), exp_w_gate[0].astype(_BF16),
             exp_w_up[0].astype(_BF16), exp_w_down[0].astype(_BF16))

    x1_p, na_p, nb_p = _mixer(x_prompt, None, mixer_w, nb=1, tt=256)
    x1_s, na_s, nb_s = _mixer(x_sample, (cache_conv_a, cache_conv_b), mixer_w, nb=16, tt=16)
    bs, ts, _ = x_sample.shape
    (y_p,) = _route_and_moe([x1_p], router_w, moe_w)
    (y_s,) = _route_and_moe([x1_s.reshape(1, bs * ts, d)], router_w, moe_w)
    return (y_p, y_s.reshape(bs, ts, d), na_p, na_s, nb_p, nb_s)
```

```python
import functools

import jax
import jax.numpy as jnp
from jax import lax
from jax.experimental import pallas as pl
from jax.experimental.pallas import tpu as pltpu

D_MODEL = 1024
K_A = 3
K_B = 31
N_GROUPS = 4
EXPERTS_PER_GROUP = 8
N_EXPERTS = N_GROUPS * EXPERTS_PER_GROUP
D_EXPERT = D_MODEL // 2
RMS_EPS = 1e-6
LN_EPS = 1e-5

SUBLANES = 8
LANES = 128
FRAME_ROWS = D_MODEL // LANES
assert FRAME_ROWS == SUBLANES
HIST_A = SUBLANES
assert HIST_A >= K_A - 1
CONV_B_SLICE = 16
W_IN_ORDER = (1, 2, 0, 5, 3, 4, 6)
N_PAIRED = 4
W_GLU_A, W_GLU_B, W_GATE_B = 4, 5, 6

VMEM_LIMIT_BYTES = 58 * 1024 * 1024

_F32 = jnp.float32
_BF16 = jnp.bfloat16


def _rms_norm(x, g):
    return x * lax.rsqrt(jnp.mean(x * x, axis=-1, keepdims=True) + RMS_EPS) * g


def _dot(a, b):
    return jnp.dot(a, b, preferred_element_type=_F32)


def _pack_rows(w):
    *lead, k, n = w.shape
    pairs = jnp.swapaxes(w.astype(_BF16).reshape(*lead, k // 2, 2, n), -1, -2)
    return lax.bitcast_convert_type(pairs, jnp.uint32)


def _wdot(a, w_packed):
    return _dot(a, pltpu.bitcast(w_packed, _BF16))


def _to_frame_major(dst_ref, first_frame, x):
    n = x.shape[0]
    for j in range(FRAME_ROWS):
        dst_ref[pl.ds(first_frame * FRAME_ROWS + j, n, stride=FRAME_ROWS), :] = (
            x[:, j * LANES:(j + 1) * LANES])


def _from_frame_major(src_ref, first_frame, n):
    return jnp.concatenate(
        [src_ref[pl.ds(first_frame * FRAME_ROWS + j, n, stride=FRAME_ROWS), :]
         for j in range(FRAME_ROWS)], axis=-1)


def _mixer_kernel(*refs, nb, tt, nt, has_cache):
    if has_cache:
        x_ref, ca_ref, cb_ref = refs[:3]
        refs = refs[3:]
    else:
        x_ref = refs[0]
        refs = refs[1:]
    (n1g_ref, win_ref, gateb_ref, caw_ref, woa_ref, cbw_ref, cbb_ref, lng_ref, lnb_ref,
     wob_ref, bob_ref, wo_ref, x1_ref, newa_ref, newb_ref,
     sa_ref, sb_ref, cv_ref, xn_ref, proj_ref) = refs
    d = D_MODEL
    m = nb * tt
    hb = K_B - 1
    seq_frames = hb + tt
    t = pl.program_id(1)
    ch = CONV_B_SLICE
    n_slices = m // ch
    assert m % (ch * N_PAIRED) == 0 and (tt % ch == 0 if nb == 1 else tt == ch)

    @pl.when(t == 0)
    def _():
        if has_cache:
            sa_ref[:, HIST_A - (K_A - 1):HIST_A, :] = ca_ref[0]
            for s in range(nb):
                _to_frame_major(sb_ref, s * seq_frames, cb_ref[0, s])
        else:
            sa_ref[:, 0:HIST_A, :] = jnp.zeros((nb, HIST_A, d), _F32)
            for s in range(nb):
                sb_ref[pl.ds(s * seq_frames * FRAME_ROWS, hb * FRAME_ROWS), :] = (
                    jnp.zeros((hb * FRAME_ROWS, LANES), _F32))

    x = x_ref[...].reshape(m, d)
    xn = _rms_norm(x, n1g_ref[...]).astype(_BF16)
    xn_ref[...] = xn

    u = _wdot(xn, win_ref[W_GLU_A]) * jax.nn.sigmoid(_wdot(xn, win_ref[W_GLU_B]))
    u3 = u.reshape(nb, tt, d)
    if tt >= hb:
        newb_ref[0] = u3[:, tt - hb:tt, :]
    else:
        newb_ref[0, :, 0:hb - tt, :] = cb_ref[0, :, tt:hb, :]
        newb_ref[0, :, hb - tt:hb, :] = u3
    for s in range(nb):
        _to_frame_major(sb_ref, s * seq_frames + hb, u[s * tt:(s + 1) * tt, :])

    def paired_body(i, carry):
        proj_ref[i] = _wdot(xn_ref[...], win_ref[i])
        for j in range(n_slices // N_PAIRED):
            q = i * (n_slices // N_PAIRED) + j
            src = q * ch if nb == 1 else q * seq_frames
            acc = None
            for k in range(K_B):
                row = pl.multiple_of((src + k) * FRAME_ROWS, FRAME_ROWS)
                win = sb_ref[pl.ds(row, ch * FRAME_ROWS), :]
                term = win.reshape(ch, FRAME_ROWS, LANES) * cbw_ref[k]
                acc = term if acc is None else acc + term
            acc = acc + cbb_ref[...]
            dst = pl.multiple_of(q * ch * FRAME_ROWS, ch * FRAME_ROWS)
            cv_ref[pl.ds(dst, ch * FRAME_ROWS), :] = acc.reshape(ch * FRAME_ROWS, LANES)
        return carry

    lax.fori_loop(0, N_PAIRED, paired_body, 0)
    c_a, h_a, b_a, ga_in = (proj_ref[i] for i in range(N_PAIRED))
    gb_in = _wdot(xn, win_ref[W_GATE_B])
    if nt > 1:
        sb_ref[pl.ds(0, hb * FRAME_ROWS), :] = sb_ref[pl.ds(tt * FRAME_ROWS, hb * FRAME_ROWS), :]

    z = c_a * h_a
    sa_ref[:, HIST_A:HIST_A + tt, :] = z.reshape(nb, tt, d)
    conv_a = None
    for k in range(K_A):
        lo = HIST_A - (K_A - 1) + k
        term = sa_ref[:, lo:lo + tt, :] * caw_ref[k:k + 1, :]
        conv_a = term if conv_a is None else conv_a + term
    newa_ref[0] = sa_ref[:, HIST_A + tt - (K_A - 1):HIST_A + tt, :]
    if nt > 1:
        sa_ref[:, HIST_A - (K_A - 1):HIST_A, :] = sa_ref[:, HIST_A + tt - (K_A - 1):HIST_A + tt, :]
    y_a = _wdot((b_a * conv_a.reshape(m, d)).astype(_BF16), woa_ref[...])

    g_a = jax.nn.sigmoid(ga_in + gateb_ref[:, 0:d])
    g_b = jax.nn.sigmoid(gb_in + gateb_ref[:, d:2 * d])

    cb = _from_frame_major(cv_ref, 0, m)
    mu = jnp.mean(cb, axis=-1, keepdims=True)
    xc = cb - mu
    var = jnp.mean(xc * xc, axis=-1, keepdims=True)
    v_b = jax.nn.silu(xc * lax.rsqrt(var + LN_EPS) * lng_ref[...] + lnb_ref[...])
    y_b = _wdot(v_b.astype(_BF16), wob_ref[...]) + bob_ref[...]

    mix = (g_a * y_a + g_b * y_b).astype(_BF16)
    x1 = x + _wdot(mix, wo_ref[...])
    x1_ref[...] = x1.reshape(nb, tt, d)


def _const_spec(shape):
    zeros = (0,) * len(shape)
    return pl.BlockSpec(shape, lambda *_: zeros, pipeline_mode=pl.Buffered(1))


def _mixer(x, caches, weights, *, nb, tt):
    b, s, d = x.shape
    assert b % nb == 0 and s % tt == 0
    nt = s // tt
    has_cache = caches is not None
    assert (nt == 1) if has_cache else (nb == 1 and tt >= K_B - 1)
    in_specs = [pl.BlockSpec((nb, tt, d), lambda i, j: (i, j, 0))]
    args = [x]
    if has_cache:
        in_specs += [pl.BlockSpec((1, nb, K_A - 1, d), lambda i, j: (0, i, 0, 0)),
                     pl.BlockSpec((1, nb, K_B - 1, d), lambda i, j: (0, i, 0, 0))]
        args += list(caches)
    in_specs += [_const_spec(w.shape) for w in weights]
    args += list(weights)
    out_shape = (jax.ShapeDtypeStruct((b, s, d), _F32),
                 jax.ShapeDtypeStruct((1, b, K_A - 1, d), _F32),
                 jax.ShapeDtypeStruct((1, b, K_B - 1, d), _F32))
    out_specs = (pl.BlockSpec((nb, tt, d), lambda i, j: (i, j, 0)),
                 pl.BlockSpec((1, nb, K_A - 1, d), lambda i, j: (0, i, 0, 0)),
                 pl.BlockSpec((1, nb, K_B - 1, d), lambda i, j: (0, i, 0, 0)))
    return pl.pallas_call(
        functools.partial(_mixer_kernel, nb=nb, tt=tt, nt=nt, has_cache=has_cache),
        grid=(b // nb, nt),
        in_specs=in_specs,
        out_specs=out_specs,
        out_shape=out_shape,
        scratch_shapes=[pltpu.VMEM((nb, HIST_A + tt, d), _F32),
                        pltpu.VMEM((nb * (K_B - 1 + tt) * FRAME_ROWS, LANES), _F32),
                        pltpu.VMEM((nb * tt * FRAME_ROWS, LANES), _F32),
                        pltpu.VMEM((nb * tt, d), _BF16),
                        pltpu.VMEM((N_PAIRED, nb * tt, d), _F32)],
        compiler_params=pltpu.CompilerParams(
            dimension_semantics=("arbitrary", "arbitrary"),
            vmem_limit_bytes=VMEM_LIMIT_BYTES),
        name="mixer_cache" if has_cache else "mixer_prompt",
    )(*args)


E_PAD = 48
PREFIX_CHUNK = 256
COUNT_SPLIT = 64


def _router_kernel(x1_ref, n2g_ref, wr_ref, rb_ref, pos_ref, wts_ref, offs_ref, *, t):
    assert t % PREFIX_CHUNK == 0 and 2 * t <= COUNT_SPLIT * 256
    xn = _rms_norm(x1_ref[0], n2g_ref[...]).astype(_BF16)
    lg = lax.dot_general(wr_ref[...], xn, (((1,), (1,)), ((), ())),
                         preferred_element_type=_F32) + rb_ref[...]
    row = [lg[r:r + 1, :] for r in range(N_GROUPS + N_EXPERTS)]

    grp = jnp.zeros((1, t), jnp.int32)
    best = row[0]
    for g in range(1, N_GROUPS):
        gt = row[g] > best
        grp = jnp.where(gt, g, grp)
        best = jnp.where(gt, row[g], best)
    denom = None
    for g in range(N_GROUPS):
        e = jnp.exp(row[g] - best)
        denom = e if denom is None else denom + e
    p_top = 1.0 / denom

    sel = []
    for j in range(EXPERTS_PER_GROUP):
        v = row[N_GROUPS + j]
        for g in range(1, N_GROUPS):
            v = jnp.where(grp == g, row[N_GROUPS + g * EXPERTS_PER_GROUP + j], v)
        sel.append(v)
    i1 = jnp.zeros((1, t), jnp.int32)
    v1 = sel[0]
    for j in range(1, EXPERTS_PER_GROUP):
        gt = sel[j] > v1
        i1 = jnp.where(gt, j, i1)
        v1 = jnp.where(gt, sel[j], v1)
    i2 = jnp.where(i1 == 0, 1, 0).astype(jnp.int32)
    v2 = jnp.where(i1 == 0, sel[1], sel[0])
    for j in range(1, EXPERTS_PER_GROUP):
        gt = (sel[j] > v2) & (i1 != j) & (i2 != j)
        i2 = jnp.where(gt, j, i2)
        v2 = jnp.where(gt, sel[j], v2)
    e2x = jnp.exp(v2 - v1)
    ssum = 1.0 + e2x
    w1 = (1.0 / ssum) * p_top
    w2 = (e2x / ssum) * p_top
    e1 = grp * EXPERTS_PER_GROUP + i1
    e2 = grp * EXPERTS_PER_GROUP + i2

    eidx = lax.broadcasted_iota(jnp.int32, (E_PAD, t), 0)
    oh1 = eidx == e1
    oh2 = eidx == e2
    hot = (oh1 | oh2).astype(_F32)
    s_i = lax.broadcasted_iota(jnp.int32, (PREFIX_CHUNK, PREFIX_CHUNK), 0)
    t_i = lax.broadcasted_iota(jnp.int32, (PREFIX_CHUNK, PREFIX_CHUNK), 1)
    before = (s_i < t_i).astype(_BF16)
    carry = jnp.zeros((E_PAD, 1), _F32)
    prefix = []
    for c in range(t // PREFIX_CHUNK):
        hc = hot[:, c * PREFIX_CHUNK:(c + 1) * PREFIX_CHUNK]
        prefix.append(_dot(hc.astype(_BF16), before) + carry)
        carry = carry + jnp.sum(hc, axis=1, keepdims=True)
    prefix = jnp.concatenate(prefix, axis=1)
    counts = jnp.broadcast_to(carry, (E_PAD, LANES))
    c_hi = jnp.floor(counts * (1.0 / COUNT_SPLIT))
    c_lo = counts - c_hi * COUNT_SPLIT
    a_i = lax.broadcasted_iota(jnp.int32, (E_PAD, E_PAD), 0)
    b_i = lax.broadcasted_iota(jnp.int32, (E_PAD, E_PAD), 1)
    lower = (b_i < a_i).astype(_BF16)
    offs = _dot(lower, c_hi.astype(_BF16)) * COUNT_SPLIT + _dot(lower, c_lo.astype(_BF16))
    offs_ref[0] = offs[:, 0:1].astype(jnp.int32)
    slot = offs[:, 0:1] + prefix
    pos1 = jnp.sum(jnp.where(oh1, slot, 0.0), axis=0, keepdims=True)
    pos2 = jnp.sum(jnp.where(oh2, slot, 0.0), axis=0, keepdims=True)
    pos_ref[0, :, 0:t] = (pos1 * FRAME_ROWS).astype(jnp.int32)
    pos_ref[0, :, t:2 * t] = (pos2 * FRAME_ROWS).astype(jnp.int32)
    wts_ref[0, :, 0:t] = w1
    wts_ref[0, :, t:2 * t] = w2


def _router(x1, n2g, wr, rb):
    nt, t, d = x1.shape
    pos, wts, offs = pl.pallas_call(
        functools.partial(_router_kernel, t=t),
        grid=(nt,),
        in_specs=[pl.BlockSpec((1, t, d), lambda k: (k, 0, 0)),
                  _const_spec(n2g.shape), _const_spec(wr.shape), _const_spec(rb.shape)],
        out_specs=(pl.BlockSpec((1, 1, 2 * t), lambda k: (k, 0, 0)),
                   pl.BlockSpec((1, 1, 2 * t), lambda k: (k, 0, 0)),
                   pl.BlockSpec((1, E_PAD, 1), lambda k: (k, 0, 0))),
        out_shape=(jax.ShapeDtypeStruct((nt, 1, 2 * t), jnp.int32),
                   jax.ShapeDtypeStruct((nt, 1, 2 * t), _F32),
                   jax.ShapeDtypeStruct((nt, E_PAD, 1), jnp.int32)),
        compiler_params=pltpu.CompilerParams(
            dimension_semantics=("arbitrary",), vmem_limit_bytes=VMEM_LIMIT_BYTES),
        name=f"router_t{t}",
    )(x1, n2g, wr, rb)
    return pos, wts, offs.reshape(nt, E_PAD)


MOE_SUB = 256
TOKENS_PER_STEP = 8
EXPERT_BLOCK = 256


def _run_copies(src_ref, src_row, dst_ref, dst_row, n, sem, *, max_rows, wait):
    for bit in reversed(range(max_rows.bit_length())):
        size = 1 << bit
        done = lax.shift_left(lax.shift_right_logical(n, bit + 1), bit + 1)

        @pl.when(lax.bitwise_and(lax.shift_right_logical(n, bit), 1) == 1)
        def _():
            s0 = pl.multiple_of((src_row + done) * FRAME_ROWS, FRAME_ROWS)
            d0 = pl.multiple_of((dst_row + done) * FRAME_ROWS, FRAME_ROWS)
            cp = pltpu.make_async_copy(src_ref.at[pl.ds(s0, size * FRAME_ROWS), :],
                                       dst_ref.at[pl.ds(d0, size * FRAME_ROWS), :], sem)
            if wait:
                cp.wait()
            else:
                cp.start()


def _scatter_kernel(offs_ref, gbase_ref, padrow_ref, padlen_ref, nused_ref, pos_ref, x1_ref, n2g_ref,
                    *refs, t, nt, n_blocks, fill, chained):
    xg_ref, xy_ref, stage_ref, zero_ref, sem, fill_sem = refs[1:] if chained else refs
    k = pl.program_id(0)
    sub = min(MOE_SUB, t)
    blk8 = EXPERT_BLOCK * FRAME_ROWS

    def sub_body(si, carry):
        r0 = pl.multiple_of(si * sub, sub)
        xn = _rms_norm(x1_ref[0, pl.ds(r0, sub), :], n2g_ref[...])
        _to_frame_major(stage_ref, 0, xn)

        def tok_body(ti, c):
            base = r0 + ti * TOKENS_PER_STEP
            for i in range(TOKENS_PER_STEP):
                row = pl.multiple_of((ti * TOKENS_PER_STEP + i) * FRAME_ROWS, FRAME_ROWS)
                v = stage_ref[pl.ds(row, FRAME_ROWS), :]
                p1 = pl.multiple_of(pos_ref[0, 0, base + i], FRAME_ROWS)
                p2 = pl.multiple_of(pos_ref[0, 0, t + base + i], FRAME_ROWS)
                xy_ref[pl.ds(p1, FRAME_ROWS), :] = v
                xy_ref[pl.ds(p2, FRAME_ROWS), :] = v
            return c

        lax.fori_loop(0, sub // TOKENS_PER_STEP, tok_body, 0)
        return carry

    lax.fori_loop(0, t // sub, sub_body, 0)

    def run_body(e, c):
        start = offs_ref[k, e]
        _run_copies(xy_ref, start, xg_ref, gbase_ref[k, e], offs_ref[k, e + 1] - start, sem,
                    max_rows=t, wait=False)
        return c

    lax.fori_loop(0, N_EXPERTS, run_body, 0)

    def fills(wait):
        def pad_body(e, c):
            _run_copies(zero_ref, 0, xg_ref, padrow_ref[e], padlen_ref[e], fill_sem,
                        max_rows=EXPERT_BLOCK - 1, wait=wait)
            return c

        def tail_body(b, c):
            cp = pltpu.make_async_copy(
                zero_ref, xg_ref.at[pl.ds(pl.multiple_of(b * blk8, blk8), blk8), :], fill_sem)
            if wait:
                cp.wait()
            else:
                cp.start()
            return c

        lax.fori_loop(0, N_EXPERTS, pad_body, 0)
        lax.fori_loop(nused_ref[0], n_blocks, tail_body, 0)

    if fill:
        @pl.when(k == nt - 1)
        def _():
            zero_ref[...] = jnp.zeros(zero_ref.shape, _F32)
            fills(wait=False)
            fills(wait=True)

    pltpu.make_async_copy(xy_ref, xg_ref.at[pl.ds(0, 2 * t * FRAME_ROWS), :], sem).wait()


def _scatter(x1, pos, n2g, offs, gbase, padrow, padlen, nused, xg_prev, *, n_blocks, fill):
    nt, t, d = x1.shape
    chained = xg_prev is not None
    n_prefetch = 5
    in_specs = [
        pl.BlockSpec((1, 1, 2 * t), lambda k, *_: (k, 0, 0), memory_space=pltpu.SMEM),
        pl.BlockSpec((1, t, d), lambda k, *_: (k, 0, 0)),
        _const_spec(n2g.shape),
    ]
    args = [offs, gbase, padrow, padlen, nused, pos, x1, n2g]
    aliases = {}
    if chained:
        in_specs.append(pl.BlockSpec(memory_space=pl.ANY))
        aliases = {len(args): 0}
        args.append(xg_prev)
    grid_spec = pltpu.PrefetchScalarGridSpec(
        num_scalar_prefetch=n_prefetch,
        grid=(nt,),
        in_specs=in_specs,
        out_specs=pl.BlockSpec(memory_space=pl.ANY),
        scratch_shapes=[pltpu.VMEM((2 * t * FRAME_ROWS, LANES), _F32),
                        pltpu.VMEM((min(MOE_SUB, t) * FRAME_ROWS, LANES), _F32),
                        pltpu.VMEM((EXPERT_BLOCK * FRAME_ROWS, LANES), _F32),
                        pltpu.SemaphoreType.DMA(()),
                        pltpu.SemaphoreType.DMA(())],
    )
    return pl.pallas_call(
        functools.partial(_scatter_kernel, t=t, nt=nt, n_blocks=n_blocks, fill=fill, chained=chained),
        grid_spec=grid_spec,
        out_shape=jax.ShapeDtypeStruct((n_blocks * EXPERT_BLOCK * FRAME_ROWS, LANES), _F32),
        input_output_aliases=aliases,
        compiler_params=pltpu.CompilerParams(
            dimension_semantics=("arbitrary",), vmem_limit_bytes=VMEM_LIMIT_BYTES),
        name=f"moe_scatter_t{t}",
    )(*args)


def _experts_kernel(bexp_ref, nused_ref, x_ref, wg_ref, wu_ref, wd_ref, y_ref,
                    wg_s, wu_s, wd_s):
    b = pl.program_id(0)

    @pl.when((b == 0) | (bexp_ref[b] != bexp_ref[jnp.maximum(b - 1, 0)]))
    def _():
        for src, dst in ((wg_ref, wg_s), (wu_ref, wu_s), (wd_ref, wd_s)):
            dst[...] = pltpu.bitcast(src[0].astype(_BF16), jnp.uint32)

    @pl.when(b < nused_ref[0])
    def _():
        xb = _from_frame_major(x_ref, 0, EXPERT_BLOCK).astype(_BF16)
        h = jax.nn.silu(_wdot(xb, wg_s[...])) * _wdot(xb, wu_s[...])
        _to_frame_major(y_ref, 0, _wdot(h.astype(_BF16), wd_s[...]))

    @pl.when(b >= nused_ref[0])
    def _():
        y_ref[...] = jnp.zeros(y_ref.shape, _F32)


def _experts(xg, bexp, nused, wg, wu, wd):
    rows8 = xg.shape[0]
    blk8 = EXPERT_BLOCK * FRAME_ROWS
    nb = rows8 // blk8
    d = D_MODEL
    grid_spec = pltpu.PrefetchScalarGridSpec(
        num_scalar_prefetch=2,
        grid=(nb,),
        in_specs=[
            pl.BlockSpec((blk8, LANES), lambda b, be, nu: (jnp.minimum(b, nu[0] - 1), 0)),
            pl.BlockSpec((1, d, D_EXPERT), lambda b, be, nu: (be[b], 0, 0)),
            pl.BlockSpec((1, d, D_EXPERT), lambda b, be, nu: (be[b], 0, 0)),
            pl.BlockSpec((1, D_EXPERT, d), lambda b, be, nu: (be[b], 0, 0)),
        ],
        out_specs=pl.BlockSpec((blk8, LANES), lambda b, be, nu: (b, 0)),
        scratch_shapes=[pltpu.VMEM((d // 2, D_EXPERT), jnp.uint32),
                        pltpu.VMEM((d // 2, D_EXPERT), jnp.uint32),
                        pltpu.VMEM((D_EXPERT // 2, d), jnp.uint32)],
    )
    return pl.pallas_call(
        _experts_kernel,
        grid_spec=grid_spec,
        out_shape=jax.ShapeDtypeStruct(xg.shape, _F32),
        compiler_params=pltpu.CompilerParams(
            dimension_semantics=("arbitrary",), vmem_limit_bytes=VMEM_LIMIT_BYTES),
        name="moe_experts",
    )(bexp, nused, xg, wg, wu, wd)


def _combine_kernel(offs_ref, gbase_ref, pos_ref, wts_ref, x1_ref, fg_ref, yg_ref, out_ref,
                    yl_ref, stage_ref, sem, *, t):
    k = pl.program_id(0)
    sub = min(MOE_SUB, t)

    def run_body(e, c):
        start = offs_ref[k, e]
        _run_copies(yg_ref, gbase_ref[k, e], yl_ref, start, offs_ref[k, e + 1] - start, sem,
                    max_rows=t, wait=False)
        return c

    lax.fori_loop(0, N_EXPERTS, run_body, 0)
    pltpu.make_async_copy(yg_ref.at[pl.ds(0, 2 * t * FRAME_ROWS), :], yl_ref, sem).wait()

    def sub_body(si, carry):
        r0 = pl.multiple_of(si * sub, sub)

        def tok_body(ti, c):
            base = r0 + ti * TOKENS_PER_STEP
            for i in range(TOKENS_PER_STEP):
                p1 = pl.multiple_of(pos_ref[0, 0, base + i], FRAME_ROWS)
                p2 = pl.multiple_of(pos_ref[0, 0, t + base + i], FRAME_ROWS)
                v = (wts_ref[0, 0, base + i] * yl_ref[pl.ds(p1, FRAME_ROWS), :]
                     + wts_ref[0, 0, t + base + i] * yl_ref[pl.ds(p2, FRAME_ROWS), :])
                row = pl.multiple_of((ti * TOKENS_PER_STEP + i) * FRAME_ROWS, FRAME_ROWS)
                stage_ref[pl.ds(row, FRAME_ROWS), :] = v
            return c

        lax.fori_loop(0, sub // TOKENS_PER_STEP, tok_body, 0)
        x2 = x1_ref[0, pl.ds(r0, sub), :] + _from_frame_major(stage_ref, 0, sub)
        out_ref[0, pl.ds(r0, sub), :] = _rms_norm(x2, fg_ref[...])
        return carry

    lax.fori_loop(0, t // sub, sub_body, 0)


def _combine(x1, pos, wts, fg, yg, offs, gbase):
    nt, t, d = x1.shape
    grid_spec = pltpu.PrefetchScalarGridSpec(
        num_scalar_prefetch=2,
        grid=(nt,),
        in_specs=[
            pl.BlockSpec((1, 1, 2 * t), lambda k, *_: (k, 0, 0), memory_space=pltpu.SMEM),
            pl.BlockSpec((1, 1, 2 * t), lambda k, *_: (k, 0, 0), memory_space=pltpu.SMEM),
            pl.BlockSpec((1, t, d), lambda k, *_: (k, 0, 0)),
            _const_spec(fg.shape),
            pl.BlockSpec(memory_space=pl.ANY),
        ],
        out_specs=pl.BlockSpec((1, t, d), lambda k, *_: (k, 0, 0)),
        scratch_shapes=[pltpu.VMEM((2 * t * FRAME_ROWS, LANES), _F32),
                        pltpu.VMEM((min(MOE_SUB, t) * FRAME_ROWS, LANES), _F32),
                        pltpu.SemaphoreType.DMA(())],
    )
    return pl.pallas_call(
        functools.partial(_combine_kernel, t=t),
        grid_spec=grid_spec,
        out_shape=jax.ShapeDtypeStruct((nt, t, d), _F32),
        compiler_params=pltpu.CompilerParams(
            dimension_semantics=("arbitrary",), vmem_limit_bytes=VMEM_LIMIT_BYTES),
        name=f"moe_combine_t{t}",
    )(offs, gbase, pos, wts, x1, fg, yg)


def _route_and_moe(x1_groups, router_w, moe_w):
    n2g, fg, wg, wu, wd = moe_w
    routed = [_router(x1, *router_w) for x1 in x1_groups]
    offs_all = jnp.concatenate([offs for _, _, offs in routed], axis=0)
    cnt = offs_all[:, 1:N_EXPERTS + 1] - offs_all[:, :N_EXPERTS]
    tot = jnp.sum(cnt, axis=0)
    nblk = (tot + (EXPERT_BLOCK - 1)) // EXPERT_BLOCK
    blk_end = jnp.cumsum(nblk)
    gstart = (blk_end - nblk) * EXPERT_BLOCK
    gbase_all = (gstart[None, :] + jnp.cumsum(cnt, axis=0) - cnt).astype(jnp.int32)
    padrow = (gstart + tot).astype(jnp.int32)
    padlen = (nblk * EXPERT_BLOCK - tot).astype(jnp.int32)
    n_rows = sum(2 * x1.shape[0] * x1.shape[1] for x1 in x1_groups)
    n_blocks = (n_rows + EXPERT_BLOCK - 1) // EXPERT_BLOCK + N_EXPERTS
    bexp = jnp.minimum(jnp.sum(jnp.arange(n_blocks)[:, None] >= blk_end[None, :], axis=1),
                       N_EXPERTS - 1).astype(jnp.int32)
    nused = blk_end[N_EXPERTS - 1:].astype(jnp.int32)
    gbases = []
    tile0 = 0
    for x1 in x1_groups:
        gbases.append(gbase_all[tile0:tile0 + x1.shape[0]])
        tile0 += x1.shape[0]
    xg = None
    for i, (x1, (pos, _, offs)) in enumerate(zip(x1_groups, routed)):
        xg = _scatter(x1, pos, n2g, offs, gbases[i], padrow, padlen, nused, xg,
                      n_blocks=n_blocks, fill=(i == len(x1_groups) - 1))
    yg = _experts(xg, bexp, nused, wg, wu, wd)
    return [_combine(x1, pos, wts, fg, yg, offs, gbase)
            for x1, (pos, wts, offs), gbase in zip(x1_groups, routed, gbases)]


def kernel(x_prompt, x_sample, cache_conv_a, cache_conv_b, norm1_g, w_in, gate_b, conv_a_w, w_out_a,
           conv_b_w, conv_b_b, ln_b_g, ln_b_b, w_out_b, b_out_b, w_o, norm2_g, router_group_w,
           router_group_b, router_expert_w, router_expert_b, exp_w_gate, exp_w_up, exp_w_down,
           final_norm_g):
    d = D_MODEL
    w_in_blocks = jnp.stack([w_in[0][:, c * d:(c + 1) * d] for c in W_IN_ORDER])
    mixer_w = (norm1_g, _pack_rows(w_in_blocks), gate_b, conv_a_w[0], _pack_rows(w_out_a[0]),
               conv_b_w[0].reshape(K_B, FRAME_ROWS, LANES), conv_b_b.reshape(FRAME_ROWS, LANES),
               ln_b_g, ln_b_b, _pack_rows(w_out_b[0]), b_out_b, _pack_rows(w_o[0]))
    n_router = N_GROUPS + N_EXPERTS
    wr = jnp.concatenate([router_group_w[0].T, router_expert_w[0].T,
                          jnp.zeros((E_PAD - n_router, d), _F32)], axis=0).astype(_BF16)
    rb = jnp.concatenate([router_group_b[0], router_expert_b[0],
                          jnp.zeros((E_PAD - n_router,), _F32)]).reshape(E_PAD, 1)
    router_w = (norm2_g, wr, rb)
    moe_w = (norm2_g, final_norm_g.reshape(1, d), exp_w_gate[0], exp_w_up[0], exp_w_down[0])

    x1_p, na_p, nb_p = _mixer(x_prompt, None, mixer_w, nb=1, tt=256)
    x1_s, na_s, nb_s = _mixer(x_sample, (cache_conv_a, cache_conv_b), mixer_w, nb=16, tt=16)
    bs, ts, _ = x_sample.shape
    (y_p,) = _route_and_moe([x1_p], router_w, moe_w)
    (y_s,) = _route_and_moe([x1_s.reshape(1, bs * ts, d)], router_w, moe_w)
    return (y_p, y_s.reshape(bs, ts, d), na_p, na_s, nb_p, nb_s)
```

```python
import functools

import jax
import jax.numpy as jnp
from jax import lax
from jax.experimental import pallas as pl
from jax.experimental.pallas import tpu as pltpu

D_MODEL = 1024
K_A = 3
K_B = 31
N_GROUPS = 4
EXPERTS_PER_GROUP = 8
N_EXPERTS = N_GROUPS * EXPERTS_PER_GROUP
D_EXPERT = D_MODEL // 2
RMS_EPS = 1e-6
LN_EPS = 1e-5

SUBLANES = 8
LANES = 128
FRAME_ROWS = D_MODEL // LANES
assert FRAME_ROWS == SUBLANES
HIST_A = SUBLANES
assert HIST_A >= K_A - 1
CONV_B_SLICE = 16
W_B_A, W_C_A, W_H_A, W_GLU_A, W_GLU_B, W_GATE_A, W_GATE_B = range(7)

VMEM_LIMIT_BYTES = 58 * 1024 * 1024

_F32 = jnp.float32
_BF16 = jnp.bfloat16


def _rms_norm(x, g):
    return x * lax.rsqrt(jnp.mean(x * x, axis=-1, keepdims=True) + RMS_EPS) * g


def _dot(a, b):
    return jnp.dot(a, b, preferred_element_type=_F32)


def _pack_rows(w):
    bits = lax.bitcast_convert_type(w.astype(_BF16), jnp.uint16).astype(jnp.uint32)
    return bits[..., 0::2, :] | (bits[..., 1::2, :] << 16)


def _wdot(a, w_packed):
    return _dot(a, pltpu.bitcast(w_packed, _BF16))


def _to_frame_major(dst_ref, first_frame, x):
    n = x.shape[0]
    for j in range(FRAME_ROWS):
        dst_ref[pl.ds(first_frame * FRAME_ROWS + j, n, stride=FRAME_ROWS), :] = (
            x[:, j * LANES:(j + 1) * LANES])


def _from_frame_major(src_ref, first_frame, n):
    return jnp.concatenate(
        [src_ref[pl.ds(first_frame * FRAME_ROWS + j, n, stride=FRAME_ROWS), :]
         for j in range(FRAME_ROWS)], axis=-1)


def _mixer_kernel(*refs, nb, tt, nt, has_cache):
    if has_cache:
        x_ref, ca_ref, cb_ref = refs[:3]
        refs = refs[3:]
    else:
        x_ref = refs[0]
        refs = refs[1:]
    (n1g_ref, win_ref, gateb_ref, caw_ref, woa_ref, cbw_ref, cbb_ref, lng_ref, lnb_ref,
     wob_ref, bob_ref, wo_ref, x1_ref, newa_ref, newb_ref, sa_ref, sb_ref, cv_ref) = refs
    d = D_MODEL
    m = nb * tt
    hb = K_B - 1
    seq_frames = hb + tt
    t = pl.program_id(1)
    ch = CONV_B_SLICE
    assert tt % ch == 0 if nb == 1 else tt == ch

    @pl.when(t == 0)
    def _():
        if has_cache:
            sa_ref[:, HIST_A - (K_A - 1):HIST_A, :] = ca_ref[0]
            for s in range(nb):
                _to_frame_major(sb_ref, s * seq_frames, cb_ref[0, s])
        else:
            sa_ref[:, 0:HIST_A, :] = jnp.zeros((nb, HIST_A, d), _F32)
            for s in range(nb):
                sb_ref[pl.ds(s * seq_frames * FRAME_ROWS, hb * FRAME_ROWS), :] = (
                    jnp.zeros((hb * FRAME_ROWS, LANES), _F32))

    x = x_ref[...].reshape(m, d)
    xn = _rms_norm(x, n1g_ref[...]).astype(_BF16)

    def in_proj(block):
        return _wdot(xn, win_ref[:, block * d:(block + 1) * d])

    u = in_proj(W_GLU_A) * jax.nn.sigmoid(in_proj(W_GLU_B))
    u3 = u.reshape(nb, tt, d)
    if tt >= hb:
        newb_ref[0] = u3[:, tt - hb:tt, :]
    else:
        newb_ref[0, :, 0:hb - tt, :] = cb_ref[0, :, tt:hb, :]
        newb_ref[0, :, hb - tt:hb, :] = u3
    for s in range(nb):
        _to_frame_major(sb_ref, s * seq_frames + hb, u[s * tt:(s + 1) * tt, :])

    for q in range(m // ch):
        src = q * ch if nb == 1 else q * seq_frames
        acc = None
        for k in range(K_B):
            win = sb_ref[pl.ds((src + k) * FRAME_ROWS, ch * FRAME_ROWS), :]
            term = win.reshape(ch, FRAME_ROWS, LANES) * cbw_ref[k]
            acc = term if acc is None else acc + term
        acc = acc + cbb_ref[...]
        cv_ref[pl.ds(q * ch * FRAME_ROWS, ch * FRAME_ROWS), :] = acc.reshape(ch * FRAME_ROWS, LANES)
    c_a, h_a, b_a, ga_in, gb_in = (in_proj(c) for c in (W_C_A, W_H_A, W_B_A, W_GATE_A, W_GATE_B))
    if nt > 1:
        sb_ref[pl.ds(0, hb * FRAME_ROWS), :] = sb_ref[pl.ds(tt * FRAME_ROWS, hb * FRAME_ROWS), :]

    z = c_a * h_a
    sa_ref[:, HIST_A:HIST_A + tt, :] = z.reshape(nb, tt, d)
    conv_a = None
    for k in range(K_A):
        lo = HIST_A - (K_A - 1) + k
        term = sa_ref[:, lo:lo + tt, :] * caw_ref[k:k + 1, :]
        conv_a = term if conv_a is None else conv_a + term
    newa_ref[0] = sa_ref[:, HIST_A + tt - (K_A - 1):HIST_A + tt, :]
    if nt > 1:
        sa_ref[:, HIST_A - (K_A - 1):HIST_A, :] = sa_ref[:, HIST_A + tt - (K_A - 1):HIST_A + tt, :]
    y_a = _wdot((b_a * conv_a.reshape(m, d)).astype(_BF16), woa_ref[...])

    g_a = jax.nn.sigmoid(ga_in + gateb_ref[:, 0:d])
    g_b = jax.nn.sigmoid(gb_in + gateb_ref[:, d:2 * d])

    cb = _from_frame_major(cv_ref, 0, m)
    mu = jnp.mean(cb, axis=-1, keepdims=True)
    xc = cb - mu
    var = jnp.mean(xc * xc, axis=-1, keepdims=True)
    v_b = jax.nn.silu(xc * lax.rsqrt(var + LN_EPS) * lng_ref[...] + lnb_ref[...])
    y_b = _wdot(v_b.astype(_BF16), wob_ref[...]) + bob_ref[...]

    mix = (g_a * y_a + g_b * y_b).astype(_BF16)
    x1 = x + _wdot(mix, wo_ref[...])
    x1_ref[...] = x1.reshape(nb, tt, d)


def _const_spec(shape):
    zeros = (0,) * len(shape)
    return pl.BlockSpec(shape, lambda *_: zeros, pipeline_mode=pl.Buffered(1))


def _mixer(x, caches, weights, *, nb, tt):
    b, s, d = x.shape
    assert b % nb == 0 and s % tt == 0
    nt = s // tt
    has_cache = caches is not None
    assert (nt == 1) if has_cache else (nb == 1 and tt >= K_B - 1)
    in_specs = [pl.BlockSpec((nb, tt, d), lambda i, j: (i, j, 0))]
    args = [x]
    if has_cache:
        in_specs += [pl.BlockSpec((1, nb, K_A - 1, d), lambda i, j: (0, i, 0, 0)),
                     pl.BlockSpec((1, nb, K_B - 1, d), lambda i, j: (0, i, 0, 0))]
        args += list(caches)
    in_specs += [_const_spec(w.shape) for w in weights]
    args += list(weights)
    out_shape = (jax.ShapeDtypeStruct((b, s, d), _F32),
                 jax.ShapeDtypeStruct((1, b, K_A - 1, d), _F32),
                 jax.ShapeDtypeStruct((1, b, K_B - 1, d), _F32))
    out_specs = (pl.BlockSpec((nb, tt, d), lambda i, j: (i, j, 0)),
                 pl.BlockSpec((1, nb, K_A - 1, d), lambda i, j: (0, i, 0, 0)),
                 pl.BlockSpec((1, nb, K_B - 1, d), lambda i, j: (0, i, 0, 0)))
    return pl.pallas_call(
        functools.partial(_mixer_kernel, nb=nb, tt=tt, nt=nt, has_cache=has_cache),
        grid=(b // nb, nt),
        in_specs=in_specs,
        out_specs=out_specs,
        out_shape=out_shape,
        scratch_shapes=[pltpu.VMEM((nb, HIST_A + tt, d), _F32),
                        pltpu.VMEM((nb * (K_B - 1 + tt) * FRAME_ROWS, LANES), _F32),
                        pltpu.VMEM((nb * tt * FRAME_ROWS, LANES), _F32)],
        compiler_params=pltpu.CompilerParams(
            dimension_semantics=("arbitrary", "arbitrary"),
            vmem_limit_bytes=VMEM_LIMIT_BYTES),
        name="mixer_cache" if has_cache else "mixer_prompt",
    )(*args)


E_PAD = 48
PREFIX_CHUNK = 256
COUNT_SPLIT = 64


def _router_kernel(x1_ref, n2g_ref, wr_ref, rb_ref, pos_ref, wts_ref, offs_ref, *, t):
    assert t % PREFIX_CHUNK == 0 and 2 * t <= COUNT_SPLIT * 256
    xn = _rms_norm(x1_ref[0], n2g_ref[...]).astype(_BF16)
    lg = lax.dot_general(wr_ref[...], xn, (((1,), (1,)), ((), ())),
                         preferred_element_type=_F32) + rb_ref[...]
    row = [lg[r:r + 1, :] for r in range(N_GROUPS + N_EXPERTS)]

    grp = jnp.zeros((1, t), jnp.int32)
    best = row[0]
    for g in range(1, N_GROUPS):
        gt = row[g] > best
        grp = jnp.where(gt, g, grp)
        best = jnp.where(gt, row[g], best)
    denom = None
    for g in range(N_GROUPS):
        e = jnp.exp(row[g] - best)
        denom = e if denom is None else denom + e
    p_top = 1.0 / denom

    sel = []
    for j in range(EXPERTS_PER_GROUP):
        v = row[N_GROUPS + j]
        for g in range(1, N_GROUPS):
            v = jnp.where(grp == g, row[N_GROUPS + g * EXPERTS_PER_GROUP + j], v)
        sel.append(v)
    i1 = jnp.zeros((1, t), jnp.int32)
    v1 = sel[0]
    for j in range(1, EXPERTS_PER_GROUP):
        gt = sel[j] > v1
        i1 = jnp.where(gt, j, i1)
        v1 = jnp.where(gt, sel[j], v1)
    i2 = jnp.where(i1 == 0, 1, 0).astype(jnp.int32)
    v2 = jnp.where(i1 == 0, sel[1], sel[0])
    for j in range(1, EXPERTS_PER_GROUP):
        gt = (sel[j] > v2) & (i1 != j) & (i2 != j)
        i2 = jnp.where(gt, j, i2)
        v2 = jnp.where(gt, sel[j], v2)
    e2x = jnp.exp(v2 - v1)
    ssum = 1.0 + e2x
    w1 = (1.0 / ssum) * p_top
    w2 = (e2x / ssum) * p_top
    e1 = grp * EXPERTS_PER_GROUP + i1
    e2 = grp * EXPERTS_PER_GROUP + i2

    eidx = lax.broadcasted_iota(jnp.int32, (E_PAD, t), 0)
    oh1 = eidx == e1
    oh2 = eidx == e2
    hot = (oh1 | oh2).astype(_F32)
    s_i = lax.broadcasted_iota(jnp.int32, (PREFIX_CHUNK, PREFIX_CHUNK), 0)
    t_i = lax.broadcasted_iota(jnp.int32, (PREFIX_CHUNK, PREFIX_CHUNK), 1)
    before = (s_i < t_i).astype(_BF16)
    carry = jnp.zeros((E_PAD, 1), _F32)
    prefix = []
    for c in range(t // PREFIX_CHUNK):
        hc = hot[:, c * PREFIX_CHUNK:(c + 1) * PREFIX_CHUNK]
        prefix.append(_dot(hc.astype(_BF16), before) + carry)
        carry = carry + jnp.sum(hc, axis=1, keepdims=True)
    prefix = jnp.concatenate(prefix, axis=1)
    counts = jnp.broadcast_to(carry, (E_PAD, LANES))
    c_hi = jnp.floor(counts * (1.0 / COUNT_SPLIT))
    c_lo = counts - c_hi * COUNT_SPLIT
    a_i = lax.broadcasted_iota(jnp.int32, (E_PAD, E_PAD), 0)
    b_i = lax.broadcasted_iota(jnp.int32, (E_PAD, E_PAD), 1)
    lower = (b_i < a_i).astype(_BF16)
    offs = _dot(lower, c_hi.astype(_BF16)) * COUNT_SPLIT + _dot(lower, c_lo.astype(_BF16))
    offs_ref[0] = offs[:, 0:1].astype(jnp.int32)
    slot = offs[:, 0:1] + prefix
    pos1 = jnp.sum(jnp.where(oh1, slot, 0.0), axis=0, keepdims=True)
    pos2 = jnp.sum(jnp.where(oh2, slot, 0.0), axis=0, keepdims=True)
    pos_ref[0, :, 0:t] = (pos1 * FRAME_ROWS).astype(jnp.int32)
    pos_ref[0, :, t:2 * t] = (pos2 * FRAME_ROWS).astype(jnp.int32)
    wts_ref[0, :, 0:t] = w1
    wts_ref[0, :, t:2 * t] = w2


def _router(x1, n2g, wr, rb):
    nt, t, d = x1.shape
    pos, wts, offs = pl.pallas_call(
        functools.partial(_router_kernel, t=t),
        grid=(nt,),
        in_specs=[pl.BlockSpec((1, t, d), lambda k: (k, 0, 0)),
                  _const_spec(n2g.shape), _const_spec(wr.shape), _const_spec(rb.shape)],
        out_specs=(pl.BlockSpec((1, 1, 2 * t), lambda k: (k, 0, 0)),
                   pl.BlockSpec((1, 1, 2 * t), lambda k: (k, 0, 0)),
                   pl.BlockSpec((1, E_PAD, 1), lambda k: (k, 0, 0))),
        out_shape=(jax.ShapeDtypeStruct((nt, 1, 2 * t), jnp.int32),
                   jax.ShapeDtypeStruct((nt, 1, 2 * t), _F32),
                   jax.ShapeDtypeStruct((nt, E_PAD, 1), jnp.int32)),
        compiler_params=pltpu.CompilerParams(
            dimension_semantics=("arbitrary",), vmem_limit_bytes=VMEM_LIMIT_BYTES),
        name=f"router_t{t}",
    )(x1, n2g, wr, rb)
    return pos, wts, offs.reshape(nt, E_PAD)


MOE_SUB = 256
TOKENS_PER_STEP = 8
EXPERT_BLOCK = 256


def _run_copies(src_ref, src_row, dst_ref, dst_row, n, sem, *, max_rows, wait):
    for bit in reversed(range(max_rows.bit_length())):
        size = 1 << bit
        done = lax.shift_left(lax.shift_right_logical(n, bit + 1), bit + 1)

        @pl.when(lax.bitwise_and(lax.shift_right_logical(n, bit), 1) == 1)
        def _():
            s0 = pl.multiple_of((src_row + done) * FRAME_ROWS, FRAME_ROWS)
            d0 = pl.multiple_of((dst_row + done) * FRAME_ROWS, FRAME_ROWS)
            cp = pltpu.make_async_copy(src_ref.at[pl.ds(s0, size * FRAME_ROWS), :],
                                       dst_ref.at[pl.ds(d0, size * FRAME_ROWS), :], sem)
            if wait:
                cp.wait()
            else:
                cp.start()


def _scatter_kernel(offs_ref, gbase_ref, padrow_ref, padlen_ref, nused_ref, pos_ref, x1_ref, n2g_ref,
                    xg_ref, xy2_ref, stage_ref, zero_ref, sems, fill_sem, *, t, nt, n_blocks):
    k = pl.program_id(0)
    sub = min(MOE_SUB, t)
    blk8 = EXPERT_BLOCK * FRAME_ROWS
    slot = lax.rem(k, 2)
    xy_ref = xy2_ref.at[slot]

    def wait_runs(s):
        pltpu.make_async_copy(xy2_ref.at[s], xg_ref.at[pl.ds(0, 2 * t * FRAME_ROWS), :],
                              sems.at[s]).wait()

    @pl.when(k >= 2)
    def _():
        wait_runs(slot)

    def sub_body(si, carry):
        r0 = pl.multiple_of(si * sub, sub)
        xn = _rms_norm(x1_ref[0, pl.ds(r0, sub), :], n2g_ref[...])
        _to_frame_major(stage_ref, 0, xn)

        def tok_body(ti, c):
            base = r0 + ti * TOKENS_PER_STEP
            for i in range(TOKENS_PER_STEP):
                row = pl.multiple_of((ti * TOKENS_PER_STEP + i) * FRAME_ROWS, FRAME_ROWS)
                v = stage_ref[pl.ds(row, FRAME_ROWS), :]
                p1 = pl.multiple_of(pos_ref[0, 0, base + i], FRAME_ROWS)
                p2 = pl.multiple_of(pos_ref[0, 0, t + base + i], FRAME_ROWS)
                xy_ref[pl.ds(p1, FRAME_ROWS), :] = v
                xy_ref[pl.ds(p2, FRAME_ROWS), :] = v
            return c

        lax.fori_loop(0, sub // TOKENS_PER_STEP, tok_body, 0)
        return carry

    lax.fori_loop(0, t // sub, sub_body, 0)

    def run_body(e, c):
        start = offs_ref[k, e]
        _run_copies(xy_ref, start, xg_ref, gbase_ref[k, e], offs_ref[k, e + 1] - start,
                    sems.at[slot], max_rows=t, wait=False)
        return c

    lax.fori_loop(0, N_EXPERTS, run_body, 0)

    def fills(wait):
        def pad_body(e, c):
            _run_copies(zero_ref, 0, xg_ref, padrow_ref[e], padlen_ref[e], fill_sem,
                        max_rows=EXPERT_BLOCK - 1, wait=wait)
            return c

        def tail_body(b, c):
            cp = pltpu.make_async_copy(
                zero_ref, xg_ref.at[pl.ds(pl.multiple_of(b * blk8, blk8), blk8), :], fill_sem)
            if wait:
                cp.wait()
            else:
                cp.start()
            return c

        lax.fori_loop(0, N_EXPERTS, pad_body, 0)
        lax.fori_loop(nused_ref[0], n_blocks, tail_body, 0)

    @pl.when(k == nt - 1)
    def _():
        zero_ref[...] = jnp.zeros(zero_ref.shape, _F32)
        fills(wait=False)
        fills(wait=True)
        wait_runs(slot)
        if nt > 1:
            wait_runs(1 - slot)


def _scatter(x1, pos, n2g, offs, gbase, padrow, padlen, nused, *, n_blocks):
    nt, t, d = x1.shape
    grid_spec = pltpu.PrefetchScalarGridSpec(
        num_scalar_prefetch=5,
        grid=(nt,),
        in_specs=[
            pl.BlockSpec((1, 1, 2 * t), lambda k, *_: (k, 0, 0), memory_space=pltpu.SMEM),
            pl.BlockSpec((1, t, d), lambda k, *_: (k, 0, 0)),
            _const_spec(n2g.shape),
        ],
        out_specs=pl.BlockSpec(memory_space=pl.ANY),
        scratch_shapes=[pltpu.VMEM((2, 2 * t * FRAME_ROWS, LANES), _F32),
                        pltpu.VMEM((min(MOE_SUB, t) * FRAME_ROWS, LANES), _F32),
                        pltpu.VMEM((EXPERT_BLOCK * FRAME_ROWS, LANES), _F32),
                        pltpu.SemaphoreType.DMA((2,)),
                        pltpu.SemaphoreType.DMA(())],
    )
    return pl.pallas_call(
        functools.partial(_scatter_kernel, t=t, nt=nt, n_blocks=n_blocks),
        grid_spec=grid_spec,
        out_shape=jax.ShapeDtypeStruct((n_blocks * EXPERT_BLOCK * FRAME_ROWS, LANES), _F32),
        compiler_params=pltpu.CompilerParams(
            dimension_semantics=("arbitrary",), vmem_limit_bytes=VMEM_LIMIT_BYTES),
        name=f"moe_scatter_t{t}",
    )(offs, gbase, padrow, padlen, nused, pos, x1, n2g)


def _experts_kernel(bexp_ref, nused_ref, x_ref, wg_ref, wu_ref, wd_ref, y_ref,
                    wg_s, wu_s, wd_s):
    b = pl.program_id(0)

    @pl.when((b == 0) | (bexp_ref[b] != bexp_ref[jnp.maximum(b - 1, 0)]))
    def _():
        for src, dst in ((wg_ref, wg_s), (wu_ref, wu_s), (wd_ref, wd_s)):
            dst[...] = pltpu.bitcast(src[0].astype(_BF16), jnp.uint32)

    @pl.when(b < nused_ref[0])
    def _():
        xb = _from_frame_major(x_ref, 0, EXPERT_BLOCK).astype(_BF16)
        h = jax.nn.silu(_wdot(xb, wg_s[...])) * _wdot(xb, wu_s[...])
        _to_frame_major(y_ref, 0, _wdot(h.astype(_BF16), wd_s[...]))

    @pl.when(b >= nused_ref[0])
    def _():
        y_ref[...] = jnp.zeros(y_ref.shape, _F32)


def _experts(xg, bexp, nused, wg, wu, wd):
    rows8 = xg.shape[0]
    blk8 = EXPERT_BLOCK * FRAME_ROWS
    nb = rows8 // blk8
    d = D_MODEL
    grid_spec = pltpu.PrefetchScalarGridSpec(
        num_scalar_prefetch=2,
        grid=(nb,),
        in_specs=[
            pl.BlockSpec((blk8, LANES), lambda b, be, nu: (jnp.minimum(b, nu[0] - 1), 0)),
            pl.BlockSpec((1, d, D_EXPERT), lambda b, be, nu: (be[b], 0, 0)),
            pl.BlockSpec((1, d, D_EXPERT), lambda b, be, nu: (be[b], 0, 0)),
            pl.BlockSpec((1, D_EXPERT, d), lambda b, be, nu: (be[b], 0, 0)),
        ],
        out_specs=pl.BlockSpec((blk8, LANES), lambda b, be, nu: (b, 0)),
        scratch_shapes=[pltpu.VMEM((d // 2, D_EXPERT), jnp.uint32),
                        pltpu.VMEM((d // 2, D_EXPERT), jnp.uint32),
                        pltpu.VMEM((D_EXPERT // 2, d), jnp.uint32)],
    )
    return pl.pallas_call(
        _experts_kernel,
        grid_spec=grid_spec,
        out_shape=jax.ShapeDtypeStruct(xg.shape, _F32),
        compiler_params=pltpu.CompilerParams(
            dimension_semantics=("arbitrary",), vmem_limit_bytes=VMEM_LIMIT_BYTES),
        name="moe_experts",
    )(bexp, nused, xg, wg, wu, wd)


def _combine_kernel(offs_ref, gbase_ref, pos_ref, wts_ref, x1_ref, fg_ref, yg_ref, out_ref,
                    yl_ref, stage_ref, sem, *, t):
    k = pl.program_id(0)
    sub = min(MOE_SUB, t)

    def run_body(e, c):
        start = offs_ref[k, e]
        _run_copies(yg_ref, gbase_ref[k, e], yl_ref, start, offs_ref[k, e + 1] - start, sem,
                    max_rows=t, wait=False)
        return c

    lax.fori_loop(0, N_EXPERTS, run_body, 0)
    pltpu.make_async_copy(yg_ref.at[pl.ds(0, 2 * t * FRAME_ROWS), :], yl_ref, sem).wait()

    def sub_body(si, carry):
        r0 = pl.multiple_of(si * sub, sub)

        def tok_body(ti, c):
            base = r0 + ti * TOKENS_PER_STEP
            for i in range(TOKENS_PER_STEP):
                p1 = pl.multiple_of(pos_ref[0, 0, base + i], FRAME_ROWS)
                p2 = pl.multiple_of(pos_ref[0, 0, t + base + i], FRAME_ROWS)
                v = (wts_ref[0, 0, base + i] * yl_ref[pl.ds(p1, FRAME_ROWS), :]
                     + wts_ref[0, 0, t + base + i] * yl_ref[pl.ds(p2, FRAME_ROWS), :])
                row = pl.multiple_of((ti * TOKENS_PER_STEP + i) * FRAME_ROWS, FRAME_ROWS)
                stage_ref[pl.ds(row, FRAME_ROWS), :] = v
            return c

        lax.fori_loop(0, sub // TOKENS_PER_STEP, tok_body, 0)
        x2 = x1_ref[0, pl.ds(r0, sub), :] + _from_frame_major(stage_ref, 0, sub)
        out_ref[0, pl.ds(r0, sub), :] = _rms_norm(x2, fg_ref[...])
        return carry

    lax.fori_loop(0, t // sub, sub_body, 0)


def _combine(x1, pos, wts, fg, yg, offs, gbase):
    nt, t, d = x1.shape
    grid_spec = pltpu.PrefetchScalarGridSpec(
        num_scalar_prefetch=2,
        grid=(nt,),
        in_specs=[
            pl.BlockSpec((1, 1, 2 * t), lambda k, *_: (k, 0, 0), memory_space=pltpu.SMEM),
            pl.BlockSpec((1, 1, 2 * t), lambda k, *_: (k, 0, 0), memory_space=pltpu.SMEM),
            pl.BlockSpec((1, t, d), lambda k, *_: (k, 0, 0)),
            _const_spec(fg.shape),
            pl.BlockSpec(memory_space=pl.ANY),
        ],
        out_specs=pl.BlockSpec((1, t, d), lambda k, *_: (k, 0, 0)),
        scratch_shapes=[pltpu.VMEM((2 * t * FRAME_ROWS, LANES), _F32),
                        pltpu.VMEM((min(MOE_SUB, t) * FRAME_ROWS, LANES), _F32),
                        pltpu.SemaphoreType.DMA(())],
    )
    return pl.pallas_call(
        functools.partial(_combine_kernel, t=t),
        grid_spec=grid_spec,
        out_shape=jax.ShapeDtypeStruct((nt, t, d), _F32),
        compiler_params=pltpu.CompilerParams(
            dimension_semantics=("arbitrary",), vmem_limit_bytes=VMEM_LIMIT_BYTES),
        name=f"moe_combine_t{t}",
    )(offs, gbase, pos, wts, x1, fg, yg)


def _route_and_moe(x1, router_w, moe_w):
    n2g, fg, wg, wu, wd = moe_w
    nt, t, _ = x1.shape
    pos, wts, offs = _router(x1, *router_w)
    cnt = offs[:, 1:N_EXPERTS + 1] - offs[:, :N_EXPERTS]
    tot = jnp.sum(cnt, axis=0)
    nblk = (tot + (EXPERT_BLOCK - 1)) // EXPERT_BLOCK
    blk_end = jnp.cumsum(nblk)
    gstart = (blk_end - nblk) * EXPERT_BLOCK
    gbase = (gstart[None, :] + jnp.cumsum(cnt, axis=0) - cnt).astype(jnp.int32)
    padrow = (gstart + tot).astype(jnp.int32)
    padlen = (nblk * EXPERT_BLOCK - tot).astype(jnp.int32)
    n_blocks = (2 * nt * t + EXPERT_BLOCK - 1) // EXPERT_BLOCK + N_EXPERTS
    bexp = jnp.minimum(jnp.sum(jnp.arange(n_blocks)[:, None] >= blk_end[None, :], axis=1),
                       N_EXPERTS - 1).astype(jnp.int32)
    nused = blk_end[N_EXPERTS - 1:].astype(jnp.int32)
    xg = _scatter(x1, pos, n2g, offs, gbase, padrow, padlen, nused, n_blocks=n_blocks)
    yg = _experts(xg, bexp, nused, wg, wu, wd)
    return _combine(x1, pos, wts, fg, yg, offs, gbase)


def kernel(x_prompt, x_sample, cache_conv_a, cache_conv_b, norm1_g, w_in, gate_b, conv_a_w, w_out_a,
           conv_b_w, conv_b_b, ln_b_g, ln_b_b, w_out_b, b_out_b, w_o, norm2_g, router_group_w,
           router_group_b, router_expert_w, router_expert_b, exp_w_gate, exp_w_up, exp_w_down,
           final_norm_g):
    d = D_MODEL
    mixer_w = (norm1_g, _pack_rows(w_in[0]), gate_b, conv_a_w[0], _pack_rows(w_out_a[0]),
               conv_b_w[0].reshape(K_B, FRAME_ROWS, LANES), conv_b_b.reshape(FRAME_ROWS, LANES),
               ln_b_g, ln_b_b, _pack_rows(w_out_b[0]), b_out_b, _pack_rows(w_o[0]))
    n_router = N_GROUPS + N_EXPERTS
    wr = jnp.concatenate([router_group_w[0].T, router_expert_w[0].T,
                          jnp.zeros((E_PAD - n_router, d), _F32)], axis=0).astype(_BF16)
    rb = jnp.concatenate([router_group_b[0], router_expert_b[0],
                          jnp.zeros((E_PAD - n_router,), _F32)]).reshape(E_PAD, 1)
    router_w = (norm2_g, wr, rb)
    moe_w = (norm2_g, final_norm_g.reshape(1, d), exp_w_gate[0], exp_w_up[0], exp_w_down[0])

    x1_p, na_p, nb_p = _mixer(x_prompt, None, mixer_w, nb=1, tt=256)
    x1_s, na_s, nb_s = _mixer(x_sample, (cache_conv_a, cache_conv_b), mixer_w, nb=16, tt=16)
    bs, ts, _ = x_sample.shape
    y_p = _route_and_moe(x1_p, router_w, moe_w)
    y_s = _route_and_moe(x1_s.reshape(1, bs * ts, d), router_w, moe_w)
    return (y_p, y_s.reshape(bs, ts, d), na_p, na_s, nb_p, nb_s)
```

```python
import functools

import jax
import jax.numpy as jnp
from jax import lax
from jax.experimental import pallas as pl
from jax.experimental.pallas import tpu as pltpu

D_MODEL = 1024
K_A = 3
K_B = 31
N_GROUPS = 4
EXPERTS_PER_GROUP = 8
N_EXPERTS = N_GROUPS * EXPERTS_PER_GROUP
D_EXPERT = D_MODEL // 2
RMS_EPS = 1e-6
LN_EPS = 1e-5

SUBLANES = 8
LANES = 128
FRAME_ROWS = D_MODEL // LANES
assert FRAME_ROWS == SUBLANES
HIST_A = SUBLANES
assert HIST_A >= K_A - 1
CONV_B_SLICE = 16
W_B_A, W_C_A, W_H_A, W_GLU_A, W_GLU_B, W_GATE_A, W_GATE_B = range(7)

VMEM_LIMIT_BYTES = 58 * 1024 * 1024

_F32 = jnp.float32
_BF16 = jnp.bfloat16


def _rms_norm(x, g):
    return x * lax.rsqrt(jnp.mean(x * x, axis=-1, keepdims=True) + RMS_EPS) * g


def _dot(a, b):
    return jnp.dot(a, b, preferred_element_type=_F32)


def _wdot(a, w_packed):
    return _dot(a, pltpu.bitcast(w_packed, _BF16))


def _to_frame_major(dst_ref, first_frame, x):
    n = x.shape[0]
    for j in range(FRAME_ROWS):
        dst_ref[pl.ds(first_frame * FRAME_ROWS + j, n, stride=FRAME_ROWS), :] = (
            x[:, j * LANES:(j + 1) * LANES])


def _from_frame_major(src_ref, first_frame, n):
    return jnp.concatenate(
        [src_ref[pl.ds(first_frame * FRAME_ROWS + j, n, stride=FRAME_ROWS), :]
         for j in range(FRAME_ROWS)], axis=-1)


def _mixer_kernel(*refs, nb, tt, nt, has_cache):
    if has_cache:
        x_ref, ca_ref, cb_ref = refs[:3]
        refs = refs[3:]
    else:
        x_ref = refs[0]
        refs = refs[1:]
    (n1g_ref, win_ref, gateb_ref, caw_ref, woa_ref, cbw_ref, cbb_ref, lng_ref, lnb_ref,
     wob_ref, bob_ref, wo_ref, x1_ref, newa_ref, newb_ref, sa_ref, sb_ref, cv_ref) = refs
    d = D_MODEL
    m = nb * tt
    hb = K_B - 1
    seq_frames = hb + tt
    t = pl.program_id(1)
    ch = CONV_B_SLICE
    assert tt % ch == 0 if nb == 1 else tt == ch

    @pl.when(t == 0)
    def _():
        if has_cache:
            sa_ref[:, HIST_A - (K_A - 1):HIST_A, :] = ca_ref[0]
            for s in range(nb):
                _to_frame_major(sb_ref, s * seq_frames, cb_ref[0, s])
        else:
            sa_ref[:, 0:HIST_A, :] = jnp.zeros((nb, HIST_A, d), _F32)
            for s in range(nb):
                sb_ref[pl.ds(s * seq_frames * FRAME_ROWS, hb * FRAME_ROWS), :] = (
                    jnp.zeros((hb * FRAME_ROWS, LANES), _F32))

    x = x_ref[...].reshape(m, d)
    xn = _rms_norm(x, n1g_ref[...]).astype(_BF16)

    def in_proj(block):
        return _dot(xn, win_ref[:, block * d:(block + 1) * d])

    u = in_proj(W_GLU_A) * jax.nn.sigmoid(in_proj(W_GLU_B))
    u3 = u.reshape(nb, tt, d)
    if tt >= hb:
        newb_ref[0] = u3[:, tt - hb:tt, :]
    else:
        newb_ref[0, :, 0:hb - tt, :] = cb_ref[0, :, tt:hb, :]
        newb_ref[0, :, hb - tt:hb, :] = u3
    for s in range(nb):
        _to_frame_major(sb_ref, s * seq_frames + hb, u[s * tt:(s + 1) * tt, :])

    for q in range(m // ch):
        src = q * ch if nb == 1 else q * seq_frames
        acc = None
        for k in range(K_B):
            win = sb_ref[pl.ds((src + k) * FRAME_ROWS, ch * FRAME_ROWS), :]
            term = win.reshape(ch, FRAME_ROWS, LANES) * cbw_ref[k]
            acc = term if acc is None else acc + term
        acc = acc + cbb_ref[...]
        cv_ref[pl.ds(q * ch * FRAME_ROWS, ch * FRAME_ROWS), :] = acc.reshape(ch * FRAME_ROWS, LANES)
    c_a, h_a, b_a, ga_in, gb_in = (in_proj(c) for c in (W_C_A, W_H_A, W_B_A, W_GATE_A, W_GATE_B))
    if nt > 1:
        sb_ref[pl.ds(0, hb * FRAME_ROWS), :] = sb_ref[pl.ds(tt * FRAME_ROWS, hb * FRAME_ROWS), :]

    z = c_a * h_a
    sa_ref[:, HIST_A:HIST_A + tt, :] = z.reshape(nb, tt, d)
    conv_a = None
    for k in range(K_A):
        lo = HIST_A - (K_A - 1) + k
        term = sa_ref[:, lo:lo + tt, :] * caw_ref[k:k + 1, :]
        conv_a = term if conv_a is None else conv_a + term
    newa_ref[0] = sa_ref[:, HIST_A + tt - (K_A - 1):HIST_A + tt, :]
    if nt > 1:
        sa_ref[:, HIST_A - (K_A - 1):HIST_A, :] = sa_ref[:, HIST_A + tt - (K_A - 1):HIST_A + tt, :]
    y_a = _dot((b_a * conv_a.reshape(m, d)).astype(_BF16), woa_ref[...])

    g_a = jax.nn.sigmoid(ga_in + gateb_ref[:, 0:d])
    g_b = jax.nn.sigmoid(gb_in + gateb_ref[:, d:2 * d])

    cb = _from_frame_major(cv_ref, 0, m)
    mu = jnp.mean(cb, axis=-1, keepdims=True)
    xc = cb - mu
    var = jnp.mean(xc * xc, axis=-1, keepdims=True)
    v_b = jax.nn.silu(xc * lax.rsqrt(var + LN_EPS) * lng_ref[...] + lnb_ref[...])
    y_b = _dot(v_b.astype(_BF16), wob_ref[...]) + bob_ref[...]

    mix = (g_a * y_a + g_b * y_b).astype(_BF16)
    x1 = x + _dot(mix, wo_ref[...])
    x1_ref[...] = x1.reshape(nb, tt, d)


def _const_spec(shape):
    zeros = (0,) * len(shape)
    return pl.BlockSpec(shape, lambda *_: zeros, pipeline_mode=pl.Buffered(1))


def _mixer(x, caches, weights, *, nb, tt):
    b, s, d = x.shape
    assert b % nb == 0 and s % tt == 0
    nt = s // tt
    has_cache = caches is not None
    assert (nt == 1) if has_cache else (nb == 1 and tt >= K_B - 1)
    in_specs = [pl.BlockSpec((nb, tt, d), lambda i, j: (i, j, 0))]
    args = [x]
    if has_cache:
        in_specs += [pl.BlockSpec((1, nb, K_A - 1, d), lambda i, j: (0, i, 0, 0)),
                     pl.BlockSpec((1, nb, K_B - 1, d), lambda i, j: (0, i, 0, 0))]
        args += list(caches)
    in_specs += [_const_spec(w.shape) for w in weights]
    args += list(weights)
    out_shape = (jax.ShapeDtypeStruct((b, s, d), _F32),
                 jax.ShapeDtypeStruct((1, b, K_A - 1, d), _F32),
                 jax.ShapeDtypeStruct((1, b, K_B - 1, d), _F32))
    out_specs = (pl.BlockSpec((nb, tt, d), lambda i, j: (i, j, 0)),
                 pl.BlockSpec((1, nb, K_A - 1, d), lambda i, j: (0, i, 0, 0)),
                 pl.BlockSpec((1, nb, K_B - 1, d), lambda i, j: (0, i, 0, 0)))
    return pl.pallas_call(
        functools.partial(_mixer_kernel, nb=nb, tt=tt, nt=nt, has_cache=has_cache),
        grid=(b // nb, nt),
        in_specs=in_specs,
        out_specs=out_specs,
        out_shape=out_shape,
        scratch_shapes=[pltpu.VMEM((nb, HIST_A + tt, d), _F32),
                        pltpu.VMEM((nb * (K_B - 1 + tt) * FRAME_ROWS, LANES), _F32),
                        pltpu.VMEM((nb * tt * FRAME_ROWS, LANES), _F32)],
        compiler_params=pltpu.CompilerParams(
            dimension_semantics=("arbitrary", "arbitrary"),
            vmem_limit_bytes=VMEM_LIMIT_BYTES),
        name="mixer_cache" if has_cache else "mixer_prompt",
    )(*args)


E_PAD = 48
PREFIX_CHUNK = 256
COUNT_SPLIT = 64


def _router_kernel(x1_ref, n2g_ref, wr_ref, rb_ref, pos_ref, wts_ref, offs_ref, *, t):
    assert t % PREFIX_CHUNK == 0 and 2 * t <= COUNT_SPLIT * 256
    xn = _rms_norm(x1_ref[0], n2g_ref[...]).astype(_BF16)
    lg = lax.dot_general(wr_ref[...], xn, (((1,), (1,)), ((), ())),
                         preferred_element_type=_F32) + rb_ref[...]
    row = [lg[r:r + 1, :] for r in range(N_GROUPS + N_EXPERTS)]

    grp = jnp.zeros((1, t), jnp.int32)
    best = row[0]
    for g in range(1, N_GROUPS):
        gt = row[g] > best
        grp = jnp.where(gt, g, grp)
        best = jnp.where(gt, row[g], best)
    denom = None
    for g in range(N_GROUPS):
        e = jnp.exp(row[g] - best)
        denom = e if denom is None else denom + e
    p_top = 1.0 / denom

    sel = []
    for j in range(EXPERTS_PER_GROUP):
        v = row[N_GROUPS + j]
        for g in range(1, N_GROUPS):
            v = jnp.where(grp == g, row[N_GROUPS + g * EXPERTS_PER_GROUP + j], v)
        sel.append(v)
    i1 = jnp.zeros((1, t), jnp.int32)
    v1 = sel[0]
    for j in range(1, EXPERTS_PER_GROUP):
        gt = sel[j] > v1
        i1 = jnp.where(gt, j, i1)
        v1 = jnp.where(gt, sel[j], v1)
    i2 = jnp.where(i1 == 0, 1, 0).astype(jnp.int32)
    v2 = jnp.where(i1 == 0, sel[1], sel[0])
    for j in range(1, EXPERTS_PER_GROUP):
        gt = (sel[j] > v2) & (i1 != j) & (i2 != j)
        i2 = jnp.where(gt, j, i2)
        v2 = jnp.where(gt, sel[j], v2)
    e2x = jnp.exp(v2 - v1)
    ssum = 1.0 + e2x
    w1 = (1.0 / ssum) * p_top
    w2 = (e2x / ssum) * p_top
    e1 = grp * EXPERTS_PER_GROUP + i1
    e2 = grp * EXPERTS_PER_GROUP + i2

    eidx = lax.broadcasted_iota(jnp.int32, (E_PAD, t), 0)
    oh1 = eidx == e1
    oh2 = eidx == e2
    hot = (oh1 | oh2).astype(_F32)
    s_i = lax.broadcasted_iota(jnp.int32, (PREFIX_CHUNK, PREFIX_CHUNK), 0)
    t_i = lax.broadcasted_iota(jnp.int32, (PREFIX_CHUNK, PREFIX_CHUNK), 1)
    before = (s_i < t_i).astype(_BF16)
    carry = jnp.zeros((E_PAD, 1), _F32)
    prefix = []
    for c in range(t // PREFIX_CHUNK):
        hc = hot[:, c * PREFIX_CHUNK:(c + 1) * PREFIX_CHUNK]
        prefix.append(_dot(hc.astype(_BF16), before) + carry)
        carry = carry + jnp.sum(hc, axis=1, keepdims=True)
    prefix = jnp.concatenate(prefix, axis=1)
    counts = jnp.broadcast_to(carry, (E_PAD, LANES))
    c_hi = jnp.floor(counts * (1.0 / COUNT_SPLIT))
    c_lo = counts - c_hi * COUNT_SPLIT
    a_i = lax.broadcasted_iota(jnp.int32, (E_PAD, E_PAD), 0)
    b_i = lax.broadcasted_iota(jnp.int32, (E_PAD, E_PAD), 1)
    lower = (b_i < a_i).astype(_BF16)
    offs = _dot(lower, c_hi.astype(_BF16)) * COUNT_SPLIT + _dot(lower, c_lo.astype(_BF16))
    offs_ref[0] = offs[:, 0:1].astype(jnp.int32)
    slot = offs[:, 0:1] + prefix
    pos1 = jnp.sum(jnp.where(oh1, slot, 0.0), axis=0, keepdims=True)
    pos2 = jnp.sum(jnp.where(oh2, slot, 0.0), axis=0, keepdims=True)
    pos_ref[0, :, 0:t] = (pos1 * FRAME_ROWS).astype(jnp.int32)
    pos_ref[0, :, t:2 * t] = (pos2 * FRAME_ROWS).astype(jnp.int32)
    wts_ref[0, :, 0:t] = w1
    wts_ref[0, :, t:2 * t] = w2


def _router(x1, n2g, wr, rb):
    nt, t, d = x1.shape
    pos, wts, offs = pl.pallas_call(
        functools.partial(_router_kernel, t=t),
        grid=(nt,),
        in_specs=[pl.BlockSpec((1, t, d), lambda k: (k, 0, 0)),
                  _const_spec(n2g.shape), _const_spec(wr.shape), _const_spec(rb.shape)],
        out_specs=(pl.BlockSpec((1, 1, 2 * t), lambda k: (k, 0, 0)),
                   pl.BlockSpec((1, 1, 2 * t), lambda k: (k, 0, 0)),
                   pl.BlockSpec((1, E_PAD, 1), lambda k: (k, 0, 0))),
        out_shape=(jax.ShapeDtypeStruct((nt, 1, 2 * t), jnp.int32),
                   jax.ShapeDtypeStruct((nt, 1, 2 * t), _F32),
                   jax.ShapeDtypeStruct((nt, E_PAD, 1), jnp.int32)),
        compiler_params=pltpu.CompilerParams(
            dimension_semantics=("arbitrary",), vmem_limit_bytes=VMEM_LIMIT_BYTES),
        name=f"router_t{t}",
    )(x1, n2g, wr, rb)
    return pos, wts, offs.reshape(nt, E_PAD)


MOE_SUB = 256
TOKENS_PER_STEP = 8
EXPERT_BLOCK = 256
COMBINE_PARTS = 2


def _run_copies(src_ref, src_row, dst_ref, dst_row, n, sem, *, max_rows, wait):
    for bit in reversed(range(max_rows.bit_length())):
        size = 1 << bit
        done = lax.shift_left(lax.shift_right_logical(n, bit + 1), bit + 1)

        @pl.when(lax.bitwise_and(lax.shift_right_logical(n, bit), 1) == 1)
        def _():
            s0 = pl.multiple_of((src_row + done) * FRAME_ROWS, FRAME_ROWS)
            d0 = pl.multiple_of((dst_row + done) * FRAME_ROWS, FRAME_ROWS)
            cp = pltpu.make_async_copy(src_ref.at[pl.ds(s0, size * FRAME_ROWS), :],
                                       dst_ref.at[pl.ds(d0, size * FRAME_ROWS), :], sem)
            if wait:
                cp.wait()
            else:
                cp.start()


def _scatter_kernel(offs_ref, gbase_ref, padrow_ref, padlen_ref, nused_ref, pos_ref, x1_ref, n2g_ref,
                    xg_ref, xy2_ref, stage_ref, zero_ref, sems, fill_sem, *, t, nt, n_blocks):
    k = pl.program_id(0)
    sub = min(MOE_SUB, t)
    blk8 = EXPERT_BLOCK * FRAME_ROWS
    slot = lax.rem(k, 2)
    xy_ref = xy2_ref.at[slot]

    def wait_runs(s):
        pltpu.make_async_copy(xy2_ref.at[s], xg_ref.at[pl.ds(0, 2 * t * FRAME_ROWS), :],
                              sems.at[s]).wait()

    @pl.when(k >= 2)
    def _():
        wait_runs(slot)

    def sub_body(si, carry):
        r0 = pl.multiple_of(si * sub, sub)
        xn = _rms_norm(x1_ref[0, pl.ds(r0, sub), :], n2g_ref[...])
        _to_frame_major(stage_ref, 0, xn)

        def tok_body(ti, c):
            base = r0 + ti * TOKENS_PER_STEP
            for i in range(TOKENS_PER_STEP):
                row = pl.multiple_of((ti * TOKENS_PER_STEP + i) * FRAME_ROWS, FRAME_ROWS)
                v = stage_ref[pl.ds(row, FRAME_ROWS), :]
                p1 = pl.multiple_of(pos_ref[0, 0, base + i], FRAME_ROWS)
                p2 = pl.multiple_of(pos_ref[0, 0, t + base + i], FRAME_ROWS)
                xy_ref[pl.ds(p1, FRAME_ROWS), :] = v
                xy_ref[pl.ds(p2, FRAME_ROWS), :] = v
            return c

        lax.fori_loop(0, sub // TOKENS_PER_STEP, tok_body, 0)
        return carry

    lax.fori_loop(0, t // sub, sub_body, 0)

    def run_body(e, c):
        start = offs_ref[k, e]
        _run_copies(xy_ref, start, xg_ref, gbase_ref[k, e], offs_ref[k, e + 1] - start,
                    sems.at[slot], max_rows=t, wait=False)
        return c

    lax.fori_loop(0, N_EXPERTS, run_body, 0)

    def fills(wait):
        def pad_body(e, c):
            _run_copies(zero_ref, 0, xg_ref, padrow_ref[e], padlen_ref[e], fill_sem,
                        max_rows=EXPERT_BLOCK - 1, wait=wait)
            return c

        def tail_body(b, c):
            cp = pltpu.make_async_copy(
                zero_ref, xg_ref.at[pl.ds(pl.multiple_of(b * blk8, blk8), blk8), :], fill_sem)
            if wait:
                cp.wait()
            else:
                cp.start()
            return c

        lax.fori_loop(0, N_EXPERTS, pad_body, 0)
        lax.fori_loop(nused_ref[0], n_blocks, tail_body, 0)

    @pl.when(k == nt - 1)
    def _():
        zero_ref[...] = jnp.zeros(zero_ref.shape, _F32)
        fills(wait=False)
        fills(wait=True)
        wait_runs(slot)
        if nt > 1:
            wait_runs(1 - slot)


def _scatter(x1, pos, n2g, offs, gbase, padrow, padlen, nused, *, n_blocks):
    nt, t, d = x1.shape
    grid_spec = pltpu.PrefetchScalarGridSpec(
        num_scalar_prefetch=5,
        grid=(nt,),
        in_specs=[
            pl.BlockSpec((1, 1, 2 * t), lambda k, *_: (k, 0, 0), memory_space=pltpu.SMEM),
            pl.BlockSpec((1, t, d), lambda k, *_: (k, 0, 0)),
            _const_spec(n2g.shape),
        ],
        out_specs=pl.BlockSpec(memory_space=pl.ANY),
        scratch_shapes=[pltpu.VMEM((2, 2 * t * FRAME_ROWS, LANES), _F32),
                        pltpu.VMEM((min(MOE_SUB, t) * FRAME_ROWS, LANES), _F32),
                        pltpu.VMEM((EXPERT_BLOCK * FRAME_ROWS, LANES), _F32),
                        pltpu.SemaphoreType.DMA((2,)),
                        pltpu.SemaphoreType.DMA(())],
    )
    return pl.pallas_call(
        functools.partial(_scatter_kernel, t=t, nt=nt, n_blocks=n_blocks),
        grid_spec=grid_spec,
        out_shape=jax.ShapeDtypeStruct((n_blocks * EXPERT_BLOCK * FRAME_ROWS, LANES), _F32),
        compiler_params=pltpu.CompilerParams(
            dimension_semantics=("arbitrary",), vmem_limit_bytes=VMEM_LIMIT_BYTES),
        name=f"moe_scatter_t{t}",
    )(offs, gbase, padrow, padlen, nused, pos, x1, n2g)


def _experts_kernel(bexp_ref, nused_ref, x_ref, wg_ref, wu_ref, wd_ref, y_ref,
                    wg_s, wu_s, wd_s):
    b = pl.program_id(0)

    @pl.when((b == 0) | (bexp_ref[b] != bexp_ref[jnp.maximum(b - 1, 0)]))
    def _():
        for src, dst in ((wg_ref, wg_s), (wu_ref, wu_s), (wd_ref, wd_s)):
            dst[...] = pltpu.bitcast(src[0].astype(_BF16), jnp.uint32)

    @pl.when(b < nused_ref[0])
    def _():
        xb = _from_frame_major(x_ref, 0, EXPERT_BLOCK).astype(_BF16)
        h = jax.nn.silu(_wdot(xb, wg_s[...])) * _wdot(xb, wu_s[...])
        _to_frame_major(y_ref, 0, _wdot(h.astype(_BF16), wd_s[...]))

    @pl.when(b >= nused_ref[0])
    def _():
        y_ref[...] = jnp.zeros(y_ref.shape, _F32)


def _experts(xg, bexp, nused, wg, wu, wd):
    rows8 = xg.shape[0]
    blk8 = EXPERT_BLOCK * FRAME_ROWS
    nb = rows8 // blk8
    d = D_MODEL
    grid_spec = pltpu.PrefetchScalarGridSpec(
        num_scalar_prefetch=2,
        grid=(nb,),
        in_specs=[
            pl.BlockSpec((blk8, LANES), lambda b, be, nu: (jnp.minimum(b, nu[0] - 1), 0)),
            pl.BlockSpec((1, d, D_EXPERT), lambda b, be, nu: (be[b], 0, 0)),
            pl.BlockSpec((1, d, D_EXPERT), lambda b, be, nu: (be[b], 0, 0)),
            pl.BlockSpec((1, D_EXPERT, d), lambda b, be, nu: (be[b], 0, 0)),
        ],
        out_specs=pl.BlockSpec((blk8, LANES), lambda b, be, nu: (b, 0)),
        scratch_shapes=[pltpu.VMEM((d // 2, D_EXPERT), jnp.uint32),
                        pltpu.VMEM((d // 2, D_EXPERT), jnp.uint32),
                        pltpu.VMEM((D_EXPERT // 2, d), jnp.uint32)],
    )
    return pl.pallas_call(
        _experts_kernel,
        grid_spec=grid_spec,
        out_shape=jax.ShapeDtypeStruct(xg.shape, _F32),
        compiler_params=pltpu.CompilerParams(
            dimension_semantics=("arbitrary",), vmem_limit_bytes=VMEM_LIMIT_BYTES),
        name="moe_experts",
    )(bexp, nused, xg, wg, wu, wd)


def _combine_kernel(offs_ref, gbase_ref, pos_ref, wts_ref, x1_ref, fg_ref, yg_ref, out_ref,
                    yl2_ref, stage_ref, sems, *, t, nt):
    k = pl.program_id(0)
    h = pl.program_id(1)
    th = t // COMBINE_PARTS
    sub = min(MOE_SUB, th)
    slot = lax.rem(k, 2)
    yl_ref = yl2_ref.at[slot]

    def start_runs(tile, s):
        def run_body(e, c):
            start = offs_ref[tile, e]
            _run_copies(yg_ref, gbase_ref[tile, e], yl2_ref.at[s], start,
                        offs_ref[tile, e + 1] - start, sems.at[s], max_rows=t, wait=False)
            return c

        lax.fori_loop(0, N_EXPERTS, run_body, 0)

    @pl.when(h == 0)
    def _():
        @pl.when(k == 0)
        def _():
            start_runs(0, 0)

        @pl.when(k + 1 < nt)
        def _():
            start_runs(k + 1, 1 - slot)

        pltpu.make_async_copy(yg_ref.at[pl.ds(0, 2 * t * FRAME_ROWS), :], yl_ref,
                              sems.at[slot]).wait()

    def sub_body(si, carry):
        r0 = pl.multiple_of(si * sub, sub)

        def tok_body(ti, c):
            base = h * th + r0 + ti * TOKENS_PER_STEP
            for i in range(TOKENS_PER_STEP):
                p1 = pl.multiple_of(pos_ref[0, 0, base + i], FRAME_ROWS)
                p2 = pl.multiple_of(pos_ref[0, 0, t + base + i], FRAME_ROWS)
                v = (wts_ref[0, 0, base + i] * yl_ref[pl.ds(p1, FRAME_ROWS), :]
                     + wts_ref[0, 0, t + base + i] * yl_ref[pl.ds(p2, FRAME_ROWS), :])
                row = pl.multiple_of((ti * TOKENS_PER_STEP + i) * FRAME_ROWS, FRAME_ROWS)
                stage_ref[pl.ds(row, FRAME_ROWS), :] = v
            return c

        lax.fori_loop(0, sub // TOKENS_PER_STEP, tok_body, 0)
        x2 = x1_ref[0, pl.ds(r0, sub), :] + _from_frame_major(stage_ref, 0, sub)
        out_ref[0, pl.ds(r0, sub), :] = _rms_norm(x2, fg_ref[...])
        return carry

    lax.fori_loop(0, th // sub, sub_body, 0)


def _combine(x1, pos, wts, fg, yg, offs, gbase):
    nt, t, d = x1.shape
    th = t // COMBINE_PARTS
    grid_spec = pltpu.PrefetchScalarGridSpec(
        num_scalar_prefetch=2,
        grid=(nt, COMBINE_PARTS),
        in_specs=[
            pl.BlockSpec((1, 1, 2 * t), lambda k, h, *_: (k, 0, 0), memory_space=pltpu.SMEM),
            pl.BlockSpec((1, 1, 2 * t), lambda k, h, *_: (k, 0, 0), memory_space=pltpu.SMEM),
            pl.BlockSpec((1, th, d), lambda k, h, *_: (k, h, 0)),
            _const_spec(fg.shape),
            pl.BlockSpec(memory_space=pl.ANY),
        ],
        out_specs=pl.BlockSpec((1, th, d), lambda k, h, *_: (k, h, 0)),
        scratch_shapes=[pltpu.VMEM((2, 2 * t * FRAME_ROWS, LANES), _F32),
                        pltpu.VMEM((min(MOE_SUB, th) * FRAME_ROWS, LANES), _F32),
                        pltpu.SemaphoreType.DMA((2,))],
    )
    return pl.pallas_call(
        functools.partial(_combine_kernel, t=t, nt=nt),
        grid_spec=grid_spec,
        out_shape=jax.ShapeDtypeStruct((nt, t, d), _F32),
        compiler_params=pltpu.CompilerParams(
            dimension_semantics=("arbitrary", "arbitrary"), vmem_limit_bytes=VMEM_LIMIT_BYTES),
        name=f"moe_combine_t{t}",
    )(offs, gbase, pos, wts, x1, fg, yg)


def _route_and_moe(x1, router_w, moe_w):
    n2g, fg, wg, wu, wd = moe_w
    nt, t, _ = x1.shape
    pos, wts, offs = _router(x1, *router_w)
    cnt = offs[:, 1:N_EXPERTS + 1] - offs[:, :N_EXPERTS]
    tot = jnp.sum(cnt, axis=0)
    nblk = (tot + (EXPERT_BLOCK - 1)) // EXPERT_BLOCK
    blk_end = jnp.cumsum(nblk)
    gstart = (blk_end - nblk) * EXPERT_BLOCK
    gbase = (gstart[None, :] + jnp.cumsum(cnt, axis=0) - cnt).astype(jnp.int32)
    padrow = (gstart + tot).astype(jnp.int32)
    padlen = (nblk * EXPERT_BLOCK - tot).astype(jnp.int32)
    n_blocks = (2 * nt * t + EXPERT_BLOCK - 1) // EXPERT_BLOCK + N_EXPERTS
    bexp = jnp.minimum(jnp.sum(jnp.arange(n_blocks)[:, None] >= blk_end[None, :], axis=1),
                       N_EXPERTS - 1).astype(jnp.int32)
    nused = blk_end[N_EXPERTS - 1:].astype(jnp.int32)
    xg = _scatter(x1, pos, n2g, offs, gbase, padrow, padlen, nused, n_blocks=n_blocks)
    yg = _experts(xg, bexp, nused, wg, wu, wd)
    return _combine(x1, pos, wts, fg, yg, offs, gbase)


def kernel(x_prompt, x_sample, cache_conv_a, cache_conv_b, norm1_g, w_in, gate_b, conv_a_w, w_out_a,
           conv_b_w, conv_b_b, ln_b_g, ln_b_b, w_out_b, b_out_b, w_o, norm2_g, router_group_w,
           router_group_b, router_expert_w, router_expert_b, exp_w_gate, exp_w_up, exp_w_down,
           final_norm_g):
    d = D_MODEL
    mixer_w = (norm1_g, w_in[0].astype(_BF16), gate_b, conv_a_w[0], w_out_a[0].astype(_BF16),
               conv_b_w[0].reshape(K_B, FRAME_ROWS, LANES), conv_b_b.reshape(FRAME_ROWS, LANES),
               ln_b_g, ln_b_b, w_out_b[0].astype(_BF16), b_out_b, w_o[0].astype(_BF16))
    n_router = N_GROUPS + N_EXPERTS
    wr = jnp.concatenate([router_group_w[0].T, router_expert_w[0].T,
                          jnp.zeros((E_PAD - n_router, d), _F32)], axis=0).astype(_BF16)
    rb = jnp.concatenate([router_group_b[0], router_expert_b[0],
                          jnp.zeros((E_PAD - n_router,), _F32)]).reshape(E_PAD, 1)
    router_w = (norm2_g, wr, rb)
    moe_w = (norm2_g, final_norm_g.reshape(1, d), exp_w_gate[0], exp_w_up[0], exp_w_down[0])

    x1_p, na_p, nb_p = _mixer(x_prompt, None, mixer_w, nb=1, tt=256)
    x1_s, na_s, nb_s = _mixer(x_sample, (cache_conv_a, cache_conv_b), mixer_w, nb=16, tt=16)
    bs, ts, _ = x_sample.shape
    y_p = _route_and_moe(x1_p, router_w, moe_w)
    y_s = _route_and_moe(x1_s.reshape(1, bs * ts, d), router_w, moe_w)
    return (y_p, y_s.reshape(bs, ts, d), na_p, na_s, nb_p, nb_s)
```

```python
import functools

import jax
import jax.numpy as jnp
from jax import lax
from jax.experimental import pallas as pl
from jax.experimental.pallas import tpu as pltpu

D_MODEL = 1024
K_A = 3
K_B = 31
N_GROUPS = 4
EXPERTS_PER_GROUP = 8
N_EXPERTS = N_GROUPS * EXPERTS_PER_GROUP
D_EXPERT = D_MODEL // 2
RMS_EPS = 1e-6
LN_EPS = 1e-5

SUBLANES = 8
LANES = 128
FRAME_ROWS = D_MODEL // LANES
assert FRAME_ROWS == SUBLANES
HIST_A = SUBLANES
assert HIST_A >= K_A - 1
CONV_B_SLICE = 16
W_B_A, W_C_A, W_H_A, W_GLU_A, W_GLU_B, W_GATE_A, W_GATE_B = range(7)

VMEM_LIMIT_BYTES = 58 * 1024 * 1024

_F32 = jnp.float32
_BF16 = jnp.bfloat16


def _rms_norm(x, g):
    return x * lax.rsqrt(jnp.mean(x * x, axis=-1, keepdims=True) + RMS_EPS) * g


def _dot(a, b):
    return jnp.dot(a, b, preferred_element_type=_F32)


def _wdot(a, w_packed):
    return _dot(a, pltpu.bitcast(w_packed, _BF16))


def _to_frame_major(dst_ref, first_frame, x):
    n = x.shape[0]
    for j in range(FRAME_ROWS):
        dst_ref[pl.ds(first_frame * FRAME_ROWS + j, n, stride=FRAME_ROWS), :] = (
            x[:, j * LANES:(j + 1) * LANES])


def _from_frame_major(src_ref, first_frame, n):
    return jnp.concatenate(
        [src_ref[pl.ds(first_frame * FRAME_ROWS + j, n, stride=FRAME_ROWS), :]
         for j in range(FRAME_ROWS)], axis=-1)


def _mixer_kernel(*refs, nb, tt, nt, has_cache):
    if has_cache:
        x_ref, ca_ref, cb_ref = refs[:3]
        refs = refs[3:]
    else:
        x_ref = refs[0]
        refs = refs[1:]
    (n1g_ref, win_ref, gateb_ref, caw_ref, woa_ref, cbw_ref, cbb_ref, lng_ref, lnb_ref,
     wob_ref, bob_ref, wo_ref, x1_ref, newa_ref, newb_ref, sa_ref, sb_ref, cv_ref) = refs
    d = D_MODEL
    m = nb * tt
    hb = K_B - 1
    seq_frames = hb + tt
    t = pl.program_id(1)
    ch = CONV_B_SLICE
    assert tt % ch == 0 if nb == 1 else tt == ch

    @pl.when(t == 0)
    def _():
        if has_cache:
            sa_ref[:, HIST_A - (K_A - 1):HIST_A, :] = ca_ref[0]
            for s in range(nb):
                _to_frame_major(sb_ref, s * seq_frames, cb_ref[0, s])
        else:
            sa_ref[:, 0:HIST_A, :] = jnp.zeros((nb, HIST_A, d), _F32)
            for s in range(nb):
                sb_ref[pl.ds(s * seq_frames * FRAME_ROWS, hb * FRAME_ROWS), :] = (
                    jnp.zeros((hb * FRAME_ROWS, LANES), _F32))

    x = x_ref[...].reshape(m, d)
    xn = _rms_norm(x, n1g_ref[...]).astype(_BF16)

    def in_proj(block):
        return _dot(xn, win_ref[:, block * d:(block + 1) * d])

    u = in_proj(W_GLU_A) * jax.nn.sigmoid(in_proj(W_GLU_B))
    u3 = u.reshape(nb, tt, d)
    if tt >= hb:
        newb_ref[0] = u3[:, tt - hb:tt, :]
    else:
        newb_ref[0, :, 0:hb - tt, :] = cb_ref[0, :, tt:hb, :]
        newb_ref[0, :, hb - tt:hb, :] = u3
    for s in range(nb):
        _to_frame_major(sb_ref, s * seq_frames + hb, u[s * tt:(s + 1) * tt, :])

    for q in range(m // ch):
        src = q * ch if nb == 1 else q * seq_frames
        acc = None
        for k in range(K_B):
            win = sb_ref[pl.ds((src + k) * FRAME_ROWS, ch * FRAME_ROWS), :]
            term = win.reshape(ch, FRAME_ROWS, LANES) * cbw_ref[k]
            acc = term if acc is None else acc + term
        acc = acc + cbb_ref[...]
        cv_ref[pl.ds(q * ch * FRAME_ROWS, ch * FRAME_ROWS), :] = acc.reshape(ch * FRAME_ROWS, LANES)
    c_a, h_a, b_a, ga_in, gb_in = (in_proj(c) for c in (W_C_A, W_H_A, W_B_A, W_GATE_A, W_GATE_B))
    if nt > 1:
        sb_ref[pl.ds(0, hb * FRAME_ROWS), :] = sb_ref[pl.ds(tt * FRAME_ROWS, hb * FRAME_ROWS), :]

    z = c_a * h_a
    sa_ref[:, HIST_A:HIST_A + tt, :] = z.reshape(nb, tt, d)
    conv_a = None
    for k in range(K_A):
        lo = HIST_A - (K_A - 1) + k
        term = sa_ref[:, lo:lo + tt, :] * caw_ref[k:k + 1, :]
        conv_a = term if conv_a is None else conv_a + term
    newa_ref[0] = sa_ref[:, HIST_A + tt - (K_A - 1):HIST_A + tt, :]
    if nt > 1:
        sa_ref[:, HIST_A - (K_A - 1):HIST_A, :] = sa_ref[:, HIST_A + tt - (K_A - 1):HIST_A + tt, :]
    y_a = _dot((b_a * conv_a.reshape(m, d)).astype(_BF16), woa_ref[...])

    g_a = jax.nn.sigmoid(ga_in + gateb_ref[:, 0:d])
    g_b = jax.nn.sigmoid(gb_in + gateb_ref[:, d:2 * d])

    cb = _from_frame_major(cv_ref, 0, m)
    mu = jnp.mean(cb, axis=-1, keepdims=True)
    xc = cb - mu
    var = jnp.mean(xc * xc, axis=-1, keepdims=True)
    v_b = jax.nn.silu(xc * lax.rsqrt(var + LN_EPS) * lng_ref[...] + lnb_ref[...])
    y_b = _dot(v_b.astype(_BF16), wob_ref[...]) + bob_ref[...]

    mix = (g_a * y_a + g_b * y_b).astype(_BF16)
    x1 = x + _dot(mix, wo_ref[...])
    x1_ref[...] = x1.reshape(nb, tt, d)


def _const_spec(shape):
    zeros = (0,) * len(shape)
    return pl.BlockSpec(shape, lambda *_: zeros, pipeline_mode=pl.Buffered(1))


def _mixer(x, caches, weights, *, nb, tt):
    b, s, d = x.shape
    assert b % nb == 0 and s % tt == 0
    nt = s // tt
    has_cache = caches is not None
    assert (nt == 1) if has_cache else (nb == 1 and tt >= K_B - 1)
    in_specs = [pl.BlockSpec((nb, tt, d), lambda i, j: (i, j, 0))]
    args = [x]
    if has_cache:
        in_specs += [pl.BlockSpec((1, nb, K_A - 1, d), lambda i, j: (0, i, 0, 0)),
                     pl.BlockSpec((1, nb, K_B - 1, d), lambda i, j: (0, i, 0, 0))]
        args += list(caches)
    in_specs += [_const_spec(w.shape) for w in weights]
    args += list(weights)
    out_shape = (jax.ShapeDtypeStruct((b, s, d), _F32),
                 jax.ShapeDtypeStruct((1, b, K_A - 1, d), _F32),
                 jax.ShapeDtypeStruct((1, b, K_B - 1, d), _F32))
    out_specs = (pl.BlockSpec((nb, tt, d), lambda i, j: (i, j, 0)),
                 pl.BlockSpec((1, nb, K_A - 1, d), lambda i, j: (0, i, 0, 0)),
                 pl.BlockSpec((1, nb, K_B - 1, d), lambda i, j: (0, i, 0, 0)))
    return pl.pallas_call(
        functools.partial(_mixer_kernel, nb=nb, tt=tt, nt=nt, has_cache=has_cache),
        grid=(b // nb, nt),
        in_specs=in_specs,
        out_specs=out_specs,
        out_shape=out_shape,
        scratch_shapes=[pltpu.VMEM((nb, HIST_A + tt, d), _F32),
                        pltpu.VMEM((nb * (K_B - 1 + tt) * FRAME_ROWS, LANES), _F32),
                        pltpu.VMEM((nb * tt * FRAME_ROWS, LANES), _F32)],
        compiler_params=pltpu.CompilerParams(
            dimension_semantics=("arbitrary", "arbitrary"),
            vmem_limit_bytes=VMEM_LIMIT_BYTES),
        name="mixer_cache" if has_cache else "mixer_prompt",
    )(*args)


E_PAD = 48
PREFIX_CHUNK = 256
COUNT_SPLIT = 64


def _router_kernel(x1_ref, n2g_ref, wr_ref, rb_ref, pos_ref, wts_ref, offs_ref, *, t):
    assert t % PREFIX_CHUNK == 0 and 2 * t <= COUNT_SPLIT * 256
    xn = _rms_norm(x1_ref[0], n2g_ref[...]).astype(_BF16)
    lg = lax.dot_general(wr_ref[...], xn, (((1,), (1,)), ((), ())),
                         preferred_element_type=_F32) + rb_ref[...]
    row = [lg[r:r + 1, :] for r in range(N_GROUPS + N_EXPERTS)]

    grp = jnp.zeros((1, t), jnp.int32)
    best = row[0]
    for g in range(1, N_GROUPS):
        gt = row[g] > best
        grp = jnp.where(gt, g, grp)
        best = jnp.where(gt, row[g], best)
    denom = None
    for g in range(N_GROUPS):
        e = jnp.exp(row[g] - best)
        denom = e if denom is None else denom + e
    p_top = 1.0 / denom

    sel = []
    for j in range(EXPERTS_PER_GROUP):
        v = row[N_GROUPS + j]
        for g in range(1, N_GROUPS):
            v = jnp.where(grp == g, row[N_GROUPS + g * EXPERTS_PER_GROUP + j], v)
        sel.append(v)
    i1 = jnp.zeros((1, t), jnp.int32)
    v1 = sel[0]
    for j in range(1, EXPERTS_PER_GROUP):
        gt = sel[j] > v1
        i1 = jnp.where(gt, j, i1)
        v1 = jnp.where(gt, sel[j], v1)
    i2 = jnp.where(i1 == 0, 1, 0).astype(jnp.int32)
    v2 = jnp.where(i1 == 0, sel[1], sel[0])
    for j in range(1, EXPERTS_PER_GROUP):
        gt = (sel[j] > v2) & (i1 != j) & (i2 != j)
        i2 = jnp.where(gt, j, i2)
        v2 = jnp.where(gt, sel[j], v2)
    e2x = jnp.exp(v2 - v1)
    ssum = 1.0 + e2x
    w1 = (1.0 / ssum) * p_top
    w2 = (e2x / ssum) * p_top
    e1 = grp * EXPERTS_PER_GROUP + i1
    e2 = grp * EXPERTS_PER_GROUP + i2

    eidx = lax.broadcasted_iota(jnp.int32, (E_PAD, t), 0)
    oh1 = eidx == e1
    oh2 = eidx == e2
    hot = (oh1 | oh2).astype(_F32)
    s_i = lax.broadcasted_iota(jnp.int32, (PREFIX_CHUNK, PREFIX_CHUNK), 0)
    t_i = lax.broadcasted_iota(jnp.int32, (PREFIX_CHUNK, PREFIX_CHUNK), 1)
    before = (s_i < t_i).astype(_BF16)
    carry = jnp.zeros((E_PAD, 1), _F32)
    prefix = []
    for c in range(t // PREFIX_CHUNK):
        hc = hot[:, c * PREFIX_CHUNK:(c + 1) * PREFIX_CHUNK]
        prefix.append(_dot(hc.astype(_BF16), before) + carry)
        carry = carry + jnp.sum(hc, axis=1, keepdims=True)
    prefix = jnp.concatenate(prefix, axis=1)
    counts = jnp.broadcast_to(carry, (E_PAD, LANES))
    c_hi = jnp.floor(counts * (1.0 / COUNT_SPLIT))
    c_lo = counts - c_hi * COUNT_SPLIT
    a_i = lax.broadcasted_iota(jnp.int32, (E_PAD, E_PAD), 0)
    b_i = lax.broadcasted_iota(jnp.int32, (E_PAD, E_PAD), 1)
    lower = (b_i < a_i).astype(_BF16)
    offs = _dot(lower, c_hi.astype(_BF16)) * COUNT_SPLIT + _dot(lower, c_lo.astype(_BF16))
    offs_ref[0] = offs[:, 0:1].astype(jnp.int32)
    slot = offs[:, 0:1] + prefix
    pos1 = jnp.sum(jnp.where(oh1, slot, 0.0), axis=0, keepdims=True)
    pos2 = jnp.sum(jnp.where(oh2, slot, 0.0), axis=0, keepdims=True)
    pos_ref[0, :, 0:t] = (pos1 * FRAME_ROWS).astype(jnp.int32)
    pos_ref[0, :, t:2 * t] = (pos2 * FRAME_ROWS).astype(jnp.int32)
    wts_ref[0, :, 0:t] = w1
    wts_ref[0, :, t:2 * t] = w2


def _router(x1, n2g, wr, rb):
    nt, t, d = x1.shape
    pos, wts, offs = pl.pallas_call(
        functools.partial(_router_kernel, t=t),
        grid=(nt,),
        in_specs=[pl.BlockSpec((1, t, d), lambda k: (k, 0, 0)),
                  _const_spec(n2g.shape), _const_spec(wr.shape), _const_spec(rb.shape)],
        out_specs=(pl.BlockSpec((1, 1, 2 * t), lambda k: (k, 0, 0)),
                   pl.BlockSpec((1, 1, 2 * t), lambda k: (k, 0, 0)),
                   pl.BlockSpec((1, E_PAD, 1), lambda k: (k, 0, 0))),
        out_shape=(jax.ShapeDtypeStruct((nt, 1, 2 * t), jnp.int32),
                   jax.ShapeDtypeStruct((nt, 1, 2 * t), _F32),
                   jax.ShapeDtypeStruct((nt, E_PAD, 1), jnp.int32)),
        compiler_params=pltpu.CompilerParams(
            dimension_semantics=("arbitrary",), vmem_limit_bytes=VMEM_LIMIT_BYTES),
        name=f"router_t{t}",
    )(x1, n2g, wr, rb)
    return pos, wts, offs.reshape(nt, E_PAD)


MOE_SUB = 256
TOKENS_PER_STEP = 8
MAX_EXPERT_BLOCK = 448
BF16_ROWS = 2 * SUBLANES


def _expert_block(n_rows):
    mean = n_rows / N_EXPERTS
    want = mean + 3.5 * mean ** 0.5
    per_expert = -(-int(want) // MAX_EXPERT_BLOCK)
    return -(-int(want / per_expert) // BF16_ROWS) * BF16_ROWS
COMBINE_PARTS = 2


def _run_copies(src_ref, src_row, dst_ref, dst_row, n, sem, *, max_rows, wait):
    for bit in reversed(range(max_rows.bit_length())):
        size = 1 << bit
        done = lax.shift_left(lax.shift_right_logical(n, bit + 1), bit + 1)

        @pl.when(lax.bitwise_and(lax.shift_right_logical(n, bit), 1) == 1)
        def _():
            s0 = pl.multiple_of((src_row + done) * FRAME_ROWS, FRAME_ROWS)
            d0 = pl.multiple_of((dst_row + done) * FRAME_ROWS, FRAME_ROWS)
            cp = pltpu.make_async_copy(src_ref.at[pl.ds(s0, size * FRAME_ROWS), :],
                                       dst_ref.at[pl.ds(d0, size * FRAME_ROWS), :], sem)
            if wait:
                cp.wait()
            else:
                cp.start()


def _scatter_kernel(offs_ref, gbase_ref, padrow_ref, padlen_ref, nused_ref, pos_ref, x1_ref, n2g_ref,
                    xg_ref, xy2_ref, stage_ref, zero_ref, sems, fill_sem, *, t, nt, n_blocks):
    k = pl.program_id(0)
    sub = min(MOE_SUB, t)
    blk8 = zero_ref.shape[0]
    slot = lax.rem(k, 2)
    xy_ref = xy2_ref.at[slot]

    def wait_runs(s):
        pltpu.make_async_copy(xy2_ref.at[s], xg_ref.at[pl.ds(0, 2 * t * FRAME_ROWS), :],
                              sems.at[s]).wait()

    @pl.when(k >= 2)
    def _():
        wait_runs(slot)

    def sub_body(si, carry):
        r0 = pl.multiple_of(si * sub, sub)
        xn = _rms_norm(x1_ref[0, pl.ds(r0, sub), :], n2g_ref[...])
        _to_frame_major(stage_ref, 0, xn)

        def tok_body(ti, c):
            base = r0 + ti * TOKENS_PER_STEP
            for i in range(TOKENS_PER_STEP):
                row = pl.multiple_of((ti * TOKENS_PER_STEP + i) * FRAME_ROWS, FRAME_ROWS)
                v = stage_ref[pl.ds(row, FRAME_ROWS), :]
                p1 = pl.multiple_of(pos_ref[0, 0, base + i], FRAME_ROWS)
                p2 = pl.multiple_of(pos_ref[0, 0, t + base + i], FRAME_ROWS)
                xy_ref[pl.ds(p1, FRAME_ROWS), :] = v
                xy_ref[pl.ds(p2, FRAME_ROWS), :] = v
            return c

        lax.fori_loop(0, sub // TOKENS_PER_STEP, tok_body, 0)
        return carry

    lax.fori_loop(0, t // sub, sub_body, 0)

    def run_body(e, c):
        start = offs_ref[k, e]
        _run_copies(xy_ref, start, xg_ref, gbase_ref[k, e], offs_ref[k, e + 1] - start,
                    sems.at[slot], max_rows=t, wait=False)
        return c

    lax.fori_loop(0, N_EXPERTS, run_body, 0)

    def fills(wait):
        def pad_body(e, c):
            _run_copies(zero_ref, 0, xg_ref, padrow_ref[e], padlen_ref[e], fill_sem,
                        max_rows=blk8 // FRAME_ROWS - 1, wait=wait)
            return c

        def tail_body(b, c):
            cp = pltpu.make_async_copy(
                zero_ref, xg_ref.at[pl.ds(pl.multiple_of(b * blk8, blk8), blk8), :], fill_sem)
            if wait:
                cp.wait()
            else:
                cp.start()
            return c

        lax.fori_loop(0, N_EXPERTS, pad_body, 0)
        lax.fori_loop(nused_ref[0], n_blocks, tail_body, 0)

    @pl.when(k == nt - 1)
    def _():
        zero_ref[...] = jnp.zeros(zero_ref.shape, _F32)
        fills(wait=False)
        fills(wait=True)
        wait_runs(slot)
        if nt > 1:
            wait_runs(1 - slot)


def _scatter(x1, pos, n2g, offs, gbase, padrow, padlen, nused, *, n_blocks, blk):
    nt, t, d = x1.shape
    grid_spec = pltpu.PrefetchScalarGridSpec(
        num_scalar_prefetch=5,
        grid=(nt,),
        in_specs=[
            pl.BlockSpec((1, 1, 2 * t), lambda k, *_: (k, 0, 0), memory_space=pltpu.SMEM),
            pl.BlockSpec((1, t, d), lambda k, *_: (k, 0, 0)),
            _const_spec(n2g.shape),
        ],
        out_specs=pl.BlockSpec(memory_space=pl.ANY),
        scratch_shapes=[pltpu.VMEM((2, 2 * t * FRAME_ROWS, LANES), _F32),
                        pltpu.VMEM((min(MOE_SUB, t) * FRAME_ROWS, LANES), _F32),
                        pltpu.VMEM((blk * FRAME_ROWS, LANES), _F32),
                        pltpu.SemaphoreType.DMA((2,)),
                        pltpu.SemaphoreType.DMA(())],
    )
    return pl.pallas_call(
        functools.partial(_scatter_kernel, t=t, nt=nt, n_blocks=n_blocks),
        grid_spec=grid_spec,
        out_shape=jax.ShapeDtypeStruct((n_blocks * blk * FRAME_ROWS, LANES), _F32),
        compiler_params=pltpu.CompilerParams(
            dimension_semantics=("arbitrary",), vmem_limit_bytes=VMEM_LIMIT_BYTES),
        name=f"moe_scatter_t{t}",
    )(offs, gbase, padrow, padlen, nused, pos, x1, n2g)


def _experts_kernel(bexp_ref, nused_ref, x_ref, wg_ref, wu_ref, wd_ref, y_ref,
                    wg_s, wu_s, wd_s):
    b = pl.program_id(0)

    @pl.when((b == 0) | (bexp_ref[b] != bexp_ref[jnp.maximum(b - 1, 0)]))
    def _():
        for src, dst in ((wg_ref, wg_s), (wu_ref, wu_s), (wd_ref, wd_s)):
            dst[...] = pltpu.bitcast(src[0].astype(_BF16), jnp.uint32)

    @pl.when(b < nused_ref[0])
    def _():
        xb = _from_frame_major(x_ref, 0, x_ref.shape[0] // FRAME_ROWS).astype(_BF16)
        h = jax.nn.silu(_wdot(xb, wg_s[...])) * _wdot(xb, wu_s[...])
        _to_frame_major(y_ref, 0, _wdot(h.astype(_BF16), wd_s[...]))

    @pl.when(b >= nused_ref[0])
    def _():
        y_ref[...] = jnp.zeros(y_ref.shape, _F32)


def _experts(xg, bexp, nused, wg, wu, wd, *, blk):
    rows8 = xg.shape[0]
    blk8 = blk * FRAME_ROWS
    nb = rows8 // blk8
    d = D_MODEL
    grid_spec = pltpu.PrefetchScalarGridSpec(
        num_scalar_prefetch=2,
        grid=(nb,),
        in_specs=[
            pl.BlockSpec((blk8, LANES), lambda b, be, nu: (jnp.minimum(b, nu[0] - 1), 0)),
            pl.BlockSpec((1, d, D_EXPERT), lambda b, be, nu: (be[b], 0, 0)),
            pl.BlockSpec((1, d, D_EXPERT), lambda b, be, nu: (be[b], 0, 0)),
            pl.BlockSpec((1, D_EXPERT, d), lambda b, be, nu: (be[b], 0, 0)),
        ],
        out_specs=pl.BlockSpec((blk8, LANES), lambda b, be, nu: (b, 0)),
        scratch_shapes=[pltpu.VMEM((d // 2, D_EXPERT), jnp.uint32),
                        pltpu.VMEM((d // 2, D_EXPERT), jnp.uint32),
                        pltpu.VMEM((D_EXPERT // 2, d), jnp.uint32)],
    )
    return pl.pallas_call(
        _experts_kernel,
        grid_spec=grid_spec,
        out_shape=jax.ShapeDtypeStruct(xg.shape, _F32),
        compiler_params=pltpu.CompilerParams(
            dimension_semantics=("arbitrary",), vmem_limit_bytes=VMEM_LIMIT_BYTES),
        name=f"moe_experts_b{blk}",
    )(bexp, nused, xg, wg, wu, wd)


def _combine_kernel(offs_ref, gbase_ref, pos_ref, wts_ref, x1_ref, fg_ref, yg_ref, out_ref,
                    yl2_ref, stage_ref, sems, *, t, nt):
    k = pl.program_id(0)
    h = pl.program_id(1)
    th = t // COMBINE_PARTS
    sub = min(MOE_SUB, th)
    slot = lax.rem(k, 2)
    yl_ref = yl2_ref.at[slot]

    def start_runs(tile, s):
        def run_body(e, c):
            start = offs_ref[tile, e]
            _run_copies(yg_ref, gbase_ref[tile, e], yl2_ref.at[s], start,
                        offs_ref[tile, e + 1] - start, sems.at[s], max_rows=t, wait=False)
            return c

        lax.fori_loop(0, N_EXPERTS, run_body, 0)

    @pl.when(h == 0)
    def _():
        @pl.when(k == 0)
        def _():
            start_runs(0, 0)

        @pl.when(k + 1 < nt)
        def _():
            start_runs(k + 1, 1 - slot)

        pltpu.make_async_copy(yg_ref.at[pl.ds(0, 2 * t * FRAME_ROWS), :], yl_ref,
                              sems.at[slot]).wait()

    def sub_body(si, carry):
        r0 = pl.multiple_of(si * sub, sub)

        def tok_body(ti, c):
            base = h * th + r0 + ti * TOKENS_PER_STEP
            for i in range(TOKENS_PER_STEP):
                p1 = pl.multiple_of(pos_ref[0, 0, base + i], FRAME_ROWS)
                p2 = pl.multiple_of(pos_ref[0, 0, t + base + i], FRAME_ROWS)
                v = (wts_ref[0, 0, base + i] * yl_ref[pl.ds(p1, FRAME_ROWS), :]
                     + wts_ref[0, 0, t + base + i] * yl_ref[pl.ds(p2, FRAME_ROWS), :])
                row = pl.multiple_of((ti * TOKENS_PER_STEP + i) * FRAME_ROWS, FRAME_ROWS)
                stage_ref[pl.ds(row, FRAME_ROWS), :] = v
            return c

        lax.fori_loop(0, sub // TOKENS_PER_STEP, tok_body, 0)
        x2 = x1_ref[0, pl.ds(r0, sub), :] + _from_frame_major(stage_ref, 0, sub)
        out_ref[0, pl.ds(r0, sub), :] = _rms_norm(x2, fg_ref[...])
        return carry

    lax.fori_loop(0, th // sub, sub_body, 0)


def _combine(x1, pos, wts, fg, yg, offs, gbase):
    nt, t, d = x1.shape
    th = t // COMBINE_PARTS
    grid_spec = pltpu.PrefetchScalarGridSpec(
        num_scalar_prefetch=2,
        grid=(nt, COMBINE_PARTS),
        in_specs=[
            pl.BlockSpec((1, 1, 2 * t), lambda k, h, *_: (k, 0, 0), memory_space=pltpu.SMEM),
            pl.BlockSpec((1, 1, 2 * t), lambda k, h, *_: (k, 0, 0), memory_space=pltpu.SMEM),
            pl.BlockSpec((1, th, d), lambda k, h, *_: (k, h, 0)),
            _const_spec(fg.shape),
            pl.BlockSpec(memory_space=pl.ANY),
        ],
        out_specs=pl.BlockSpec((1, th, d), lambda k, h, *_: (k, h, 0)),
        scratch_shapes=[pltpu.VMEM((2, 2 * t * FRAME_ROWS, LANES), _F32),
                        pltpu.VMEM((min(MOE_SUB, th) * FRAME_ROWS, LANES), _F32),
                        pltpu.SemaphoreType.DMA((2,))],
    )
    return pl.pallas_call(
        functools.partial(_combine_kernel, t=t, nt=nt),
        grid_spec=grid_spec,
        out_shape=jax.ShapeDtypeStruct((nt, t, d), _F32),
        compiler_params=pltpu.CompilerParams(
            dimension_semantics=("arbitrary", "arbitrary"), vmem_limit_bytes=VMEM_LIMIT_BYTES),
        name=f"moe_combine_t{t}",
    )(offs, gbase, pos, wts, x1, fg, yg)


def _route_and_moe(x1, router_w, moe_w):
    n2g, fg, wg, wu, wd = moe_w
    nt, t, _ = x1.shape
    pos, wts, offs = _router(x1, *router_w)
    blk = _expert_block(2 * nt * t)
    cnt = offs[:, 1:N_EXPERTS + 1] - offs[:, :N_EXPERTS]
    tot = jnp.sum(cnt, axis=0)
    nblk = (tot + (blk - 1)) // blk
    blk_end = jnp.cumsum(nblk)
    gstart = (blk_end - nblk) * blk
    gbase = (gstart[None, :] + jnp.cumsum(cnt, axis=0) - cnt).astype(jnp.int32)
    padrow = (gstart + tot).astype(jnp.int32)
    padlen = (nblk * blk - tot).astype(jnp.int32)
    n_blocks = (2 * nt * t + blk - 1) // blk + N_EXPERTS
    bexp = jnp.minimum(jnp.sum(jnp.arange(n_blocks)[:, None] >= blk_end[None, :], axis=1),
                       N_EXPERTS - 1).astype(jnp.int32)
    nused = blk_end[N_EXPERTS - 1:].astype(jnp.int32)
    xg = _scatter(x1, pos, n2g, offs, gbase, padrow, padlen, nused, n_blocks=n_blocks, blk=blk)
    yg = _experts(xg, bexp, nused, wg, wu, wd, blk=blk)
    return _combine(x1, pos, wts, fg, yg, offs, gbase)


def kernel(x_prompt, x_sample, cache_conv_a, cache_conv_b, norm1_g, w_in, gate_b, conv_a_w, w_out_a,
           conv_b_w, conv_b_b, ln_b_g, ln_b_b, w_out_b, b_out_b, w_o, norm2_g, router_group_w,
           router_group_b, router_expert_w, router_expert_b, exp_w_gate, exp_w_up, exp_w_down,
           final_norm_g):
    d = D_MODEL
    mixer_w = (norm1_g, w_in[0].astype(_BF16), gate_b, conv_a_w[0], w_out_a[0].astype(_BF16),
               conv_b_w[0].reshape(K_B, FRAME_ROWS, LANES), conv_b_b.reshape(FRAME_ROWS, LANES),
               ln_b_g, ln_b_b, w_out_b[0].astype(_BF16), b_out_b, w_o[0].astype(_BF16))
    n_router = N_GROUPS + N_EXPERTS
    wr = jnp.concatenate([router_group_w[0].T, router_expert_w[0].T,
                          jnp.zeros((E_PAD - n_router, d), _F32)], axis=0).astype(_BF16)
    rb = jnp.concatenate([router_group_b[0], router_expert_b[0],
                          jnp.zeros((E_PAD - n_router,), _F32)]).reshape(E_PAD, 1)
    router_w = (norm2_g, wr, rb)
    moe_w = (norm2_g, final_norm_g.reshape(1, d), exp_w_gate[0], exp_w_up[0], exp_w_down[0])

    x1_p, na_p, nb_p = _mixer(x_prompt, None, mixer_w, nb=1, tt=256)
    x1_s, na_s, nb_s = _mixer(x_sample, (cache_conv_a, cache_conv_b), mixer_w, nb=16, tt=16)
    bs, ts, _ = x_sample.shape
    y_p = _route_and_moe(x1_p, router_w, moe_w)
    y_s = _route_and_moe(x1_s.reshape(1, bs * ts, d), router_w, moe_w)
    return (y_p, y_s.reshape(bs, ts, d), na_p, na_s, nb_p, nb_s)
```

```python
import functools

import jax
import jax.numpy as jnp
from jax import lax
from jax.experimental import pallas as pl
from jax.experimental.pallas import tpu as pltpu

D_MODEL = 1024
K_A = 3
K_B = 31
N_GROUPS = 4
EXPERTS_PER_GROUP = 8
N_EXPERTS = N_GROUPS * EXPERTS_PER_GROUP
D_EXPERT = D_MODEL // 2
RMS_EPS = 1e-6
LN_EPS = 1e-5

SUBLANES = 8
LANES = 128
FRAME_ROWS = D_MODEL // LANES
assert FRAME_ROWS == SUBLANES
HIST_A = SUBLANES
assert HIST_A >= K_A - 1
CONV_B_SLICE = 16
W_B_A, W_C_A, W_H_A, W_GLU_A, W_GLU_B, W_GATE_A, W_GATE_B = range(7)

VMEM_LIMIT_BYTES = 58 * 1024 * 1024

_F32 = jnp.float32
_BF16 = jnp.bfloat16


def _rms_norm(x, g):
    return x * lax.rsqrt(jnp.mean(x * x, axis=-1, keepdims=True) + RMS_EPS) * g


def _dot(a, b):
    return jnp.dot(a, b, preferred_element_type=_F32)


def _wdot(a, w_packed):
    return _dot(a, pltpu.bitcast(w_packed, _BF16))


def _to_frame_major(dst_ref, first_frame, x):
    n = x.shape[0]
    for j in range(FRAME_ROWS):
        dst_ref[pl.ds(first_frame * FRAME_ROWS + j, n, stride=FRAME_ROWS), :] = (
            x[:, j * LANES:(j + 1) * LANES])


def _from_frame_major(src_ref, first_frame, n):
    return jnp.concatenate(
        [src_ref[pl.ds(first_frame * FRAME_ROWS + j, n, stride=FRAME_ROWS), :]
         for j in range(FRAME_ROWS)], axis=-1)


def _mixer_kernel(*refs, nb, tt, nt, has_cache):
    if has_cache:
        x_ref, ca_ref, cb_ref = refs[:3]
        refs = refs[3:]
    else:
        x_ref = refs[0]
        refs = refs[1:]
    (n1g_ref, win_ref, gateb_ref, caw_ref, woa_ref, cbw_ref, cbb_ref, lng_ref, lnb_ref,
     wob_ref, bob_ref, wo_ref, x1_ref, newa_ref, newb_ref, sa_ref, sb_ref, cv_ref) = refs
    d = D_MODEL
    m = nb * tt
    hb = K_B - 1
    seq_frames = hb + tt
    t = pl.program_id(1)
    ch = CONV_B_SLICE
    assert tt % ch == 0 if nb == 1 else tt == ch

    @pl.when(t == 0)
    def _():
        if has_cache:
            sa_ref[:, HIST_A - (K_A - 1):HIST_A, :] = ca_ref[0]
            for s in range(nb):
                _to_frame_major(sb_ref, s * seq_frames, cb_ref[0, s])
        else:
            sa_ref[:, 0:HIST_A, :] = jnp.zeros((nb, HIST_A, d), _F32)
            for s in range(nb):
                sb_ref[pl.ds(s * seq_frames * FRAME_ROWS, hb * FRAME_ROWS), :] = (
                    jnp.zeros((hb * FRAME_ROWS, LANES), _F32))

    x = x_ref[...].reshape(m, d)
    xn = _rms_norm(x, n1g_ref[...]).astype(_BF16)

    def in_proj(block):
        return _dot(xn, win_ref[:, block * d:(block + 1) * d])

    u = in_proj(W_GLU_A) * jax.nn.sigmoid(in_proj(W_GLU_B))
    u3 = u.reshape(nb, tt, d)
    if tt >= hb:
        newb_ref[0] = u3[:, tt - hb:tt, :]
    else:
        newb_ref[0, :, 0:hb - tt, :] = cb_ref[0, :, tt:hb, :]
        newb_ref[0, :, hb - tt:hb, :] = u3
    for s in range(nb):
        _to_frame_major(sb_ref, s * seq_frames + hb, u[s * tt:(s + 1) * tt, :])

    for q in range(m // ch):
        src = q * ch if nb == 1 else q * seq_frames
        acc = None
        for k in range(K_B):
            win = sb_ref[pl.ds((src + k) * FRAME_ROWS, ch * FRAME_ROWS), :]
            term = win.reshape(ch, FRAME_ROWS, LANES) * cbw_ref[k]
            acc = term if acc is None else acc + term
        acc = acc + cbb_ref[...]
        cv_ref[pl.ds(q * ch * FRAME_ROWS, ch * FRAME_ROWS), :] = acc.reshape(ch * FRAME_ROWS, LANES)
    c_a, h_a, b_a, ga_in, gb_in = (in_proj(c) for c in (W_C_A, W_H_A, W_B_A, W_GATE_A, W_GATE_B))
    if nt > 1:
        sb_ref[pl.ds(0, hb * FRAME_ROWS), :] = sb_ref[pl.ds(tt * FRAME_ROWS, hb * FRAME_ROWS), :]

    z = c_a * h_a
    sa_ref[:, HIST_A:HIST_A + tt, :] = z.reshape(nb, tt, d)
    conv_a = None
    for k in range(K_A):
        lo = HIST_A - (K_A - 1) + k
        term = sa_ref[:, lo:lo + tt, :] * caw_ref[k:k + 1, :]
        conv_a = term if conv_a is None else conv_a + term
    newa_ref[0] = sa_ref[:, HIST_A + tt - (K_A - 1):HIST_A + tt, :]
    if nt > 1:
        sa_ref[:, HIST_A - (K_A - 1):HIST_A, :] = sa_ref[:, HIST_A + tt - (K_A - 1):HIST_A + tt, :]
    y_a = _dot((b_a * conv_a.reshape(m, d)).astype(_BF16), woa_ref[...])

    g_a = jax.nn.sigmoid(ga_in + gateb_ref[:, 0:d])
    g_b = jax.nn.sigmoid(gb_in + gateb_ref[:, d:2 * d])

    cb = _from_frame_major(cv_ref, 0, m)
    mu = jnp.mean(cb, axis=-1, keepdims=True)
    xc = cb - mu
    var = jnp.mean(xc * xc, axis=-1, keepdims=True)
    v_b = jax.nn.silu(xc * lax.rsqrt(var + LN_EPS) * lng_ref[...] + lnb_ref[...])
    y_b = _dot(v_b.astype(_BF16), wob_ref[...]) + bob_ref[...]

    mix = (g_a * y_a + g_b * y_b).astype(_BF16)
    x1 = x + _dot(mix, wo_ref[...])
    x1_ref[...] = x1.reshape(nb, tt, d)


def _const_spec(shape):
    zeros = (0,) * len(shape)
    return pl.BlockSpec(shape, lambda *_: zeros, pipeline_mode=pl.Buffered(1))


def _mixer(x, caches, weights, *, nb, tt):
    b, s, d = x.shape
    assert b % nb == 0 and s % tt == 0
    nt = s // tt
    has_cache = caches is not None
    assert (nt == 1) if has_cache else (nb == 1 and tt >= K_B - 1)
    in_specs = [pl.BlockSpec((nb, tt, d), lambda i, j: (i, j, 0))]
    args = [x]
    if has_cache:
        in_specs += [pl.BlockSpec((1, nb, K_A - 1, d), lambda i, j: (0, i, 0, 0)),
                     pl.BlockSpec((1, nb, K_B - 1, d), lambda i, j: (0, i, 0, 0))]
        args += list(caches)
    in_specs += [_const_spec(w.shape) for w in weights]
    args += list(weights)
    out_shape = (jax.ShapeDtypeStruct((b, s, d), _F32),
                 jax.ShapeDtypeStruct((1, b, K_A - 1, d), _F32),
                 jax.ShapeDtypeStruct((1, b, K_B - 1, d), _F32))
    out_specs = (pl.BlockSpec((nb, tt, d), lambda i, j: (i, j, 0)),
                 pl.BlockSpec((1, nb, K_A - 1, d), lambda i, j: (0, i, 0, 0)),
                 pl.BlockSpec((1, nb, K_B - 1, d), lambda i, j: (0, i, 0, 0)))
    return pl.pallas_call(
        functools.partial(_mixer_kernel, nb=nb, tt=tt, nt=nt, has_cache=has_cache),
        grid=(b // nb, nt),
        in_specs=in_specs,
        out_specs=out_specs,
        out_shape=out_shape,
        scratch_shapes=[pltpu.VMEM((nb, HIST_A + tt, d), _F32),
                        pltpu.VMEM((nb * (K_B - 1 + tt) * FRAME_ROWS, LANES), _F32),
                        pltpu.VMEM((nb * tt * FRAME_ROWS, LANES), _F32)],
        compiler_params=pltpu.CompilerParams(
            dimension_semantics=("arbitrary", "arbitrary"),
            vmem_limit_bytes=VMEM_LIMIT_BYTES),
        name="mixer_cache" if has_cache else "mixer_prompt",
    )(*args)


E_PAD = 48
PREFIX_CHUNK = 256
COUNT_SPLIT = 64


def _router_kernel(x1_ref, n2g_ref, wr_ref, rb_ref, pos_ref, wts_ref, offs_ref, *, t):
    assert t % PREFIX_CHUNK == 0 and 2 * t <= COUNT_SPLIT * 256
    xn = _rms_norm(x1_ref[0], n2g_ref[...]).astype(_BF16)
    lg = lax.dot_general(wr_ref[...], xn, (((1,), (1,)), ((), ())),
                         preferred_element_type=_F32) + rb_ref[...]
    row = [lg[r:r + 1, :] for r in range(N_GROUPS + N_EXPERTS)]

    grp = jnp.zeros((1, t), jnp.int32)
    best = row[0]
    for g in range(1, N_GROUPS):
        gt = row[g] > best
        grp = jnp.where(gt, g, grp)
        best = jnp.where(gt, row[g], best)
    denom = None
    for g in range(N_GROUPS):
        e = jnp.exp(row[g] - best)
        denom = e if denom is None else denom + e
    p_top = 1.0 / denom

    sel = []
    for j in range(EXPERTS_PER_GROUP):
        v = row[N_GROUPS + j]
        for g in range(1, N_GROUPS):
            v = jnp.where(grp == g, row[N_GROUPS + g * EXPERTS_PER_GROUP + j], v)
        sel.append(v)
    i1 = jnp.zeros((1, t), jnp.int32)
    v1 = sel[0]
    for j in range(1, EXPERTS_PER_GROUP):
        gt = sel[j] > v1
        i1 = jnp.where(gt, j, i1)
        v1 = jnp.where(gt, sel[j], v1)
    i2 = jnp.where(i1 == 0, 1, 0).astype(jnp.int32)
    v2 = jnp.where(i1 == 0, sel[1], sel[0])
    for j in range(1, EXPERTS_PER_GROUP):
        gt = (sel[j] > v2) & (i1 != j) & (i2 != j)
        i2 = jnp.where(gt, j, i2)
        v2 = jnp.where(gt, sel[j], v2)
    e2x = jnp.exp(v2 - v1)
    ssum = 1.0 + e2x
    w1 = (1.0 / ssum) * p_top
    w2 = (e2x / ssum) * p_top
    e1 = grp * EXPERTS_PER_GROUP + i1
    e2 = grp * EXPERTS_PER_GROUP + i2

    eidx = lax.broadcasted_iota(jnp.int32, (E_PAD, t), 0)
    oh1 = eidx == e1
    oh2 = eidx == e2
    hot = (oh1 | oh2).astype(_F32)
    s_i = lax.broadcasted_iota(jnp.int32, (PREFIX_CHUNK, PREFIX_CHUNK), 0)
    t_i = lax.broadcasted_iota(jnp.int32, (PREFIX_CHUNK, PREFIX_CHUNK), 1)
    before = (s_i < t_i).astype(_BF16)
    carry = jnp.zeros((E_PAD, 1), _F32)
    prefix = []
    for c in range(t // PREFIX_CHUNK):
        hc = hot[:, c * PREFIX_CHUNK:(c + 1) * PREFIX_CHUNK]
        prefix.append(_dot(hc.astype(_BF16), before) + carry)
        carry = carry + jnp.sum(hc, axis=1, keepdims=True)
    prefix = jnp.concatenate(prefix, axis=1)
    counts = jnp.broadcast_to(carry, (E_PAD, LANES))
    c_hi = jnp.floor(counts * (1.0 / COUNT_SPLIT))
    c_lo = counts - c_hi * COUNT_SPLIT
    a_i = lax.broadcasted_iota(jnp.int32, (E_PAD, E_PAD), 0)
    b_i = lax.broadcasted_iota(jnp.int32, (E_PAD, E_PAD), 1)
    lower = (b_i < a_i).astype(_BF16)
    offs = _dot(lower, c_hi.astype(_BF16)) * COUNT_SPLIT + _dot(lower, c_lo.astype(_BF16))
    offs_ref[0] = offs[:, 0:1].astype(jnp.int32)
    slot = offs[:, 0:1] + prefix
    pos1 = jnp.sum(jnp.where(oh1, slot, 0.0), axis=0, keepdims=True)
    pos2 = jnp.sum(jnp.where(oh2, slot, 0.0), axis=0, keepdims=True)
    pos_ref[0, :, 0:t] = (pos1 * FRAME_ROWS).astype(jnp.int32)
    pos_ref[0, :, t:2 * t] = (pos2 * FRAME_ROWS).astype(jnp.int32)
    wts_ref[0, :, 0:t] = w1
    wts_ref[0, :, t:2 * t] = w2


def _router(x1, n2g, wr, rb):
    nt, t, d = x1.shape
    pos, wts, offs = pl.pallas_call(
        functools.partial(_router_kernel, t=t),
        grid=(nt,),
        in_specs=[pl.BlockSpec((1, t, d), lambda k: (k, 0, 0)),
                  _const_spec(n2g.shape), _const_spec(wr.shape), _const_spec(rb.shape)],
        out_specs=(pl.BlockSpec((1, 1, 2 * t), lambda k: (k, 0, 0)),
                   pl.BlockSpec((1, 1, 2 * t), lambda k: (k, 0, 0)),
                   pl.BlockSpec((1, E_PAD, 1), lambda k: (k, 0, 0))),
        out_shape=(jax.ShapeDtypeStruct((nt, 1, 2 * t), jnp.int32),
                   jax.ShapeDtypeStruct((nt, 1, 2 * t), _F32),
                   jax.ShapeDtypeStruct((nt, E_PAD, 1), jnp.int32)),
        compiler_params=pltpu.CompilerParams(
            dimension_semantics=("arbitrary",), vmem_limit_bytes=VMEM_LIMIT_BYTES),
        name=f"router_t{t}",
    )(x1, n2g, wr, rb)
    return pos, wts, offs.reshape(nt, E_PAD)


MOE_SUB = 256
TOKENS_PER_STEP = 8
MAX_EXPERT_BLOCK = 448
BF16_ROWS = 2 * SUBLANES


def _expert_block(n_rows):
    mean = n_rows / N_EXPERTS
    want = mean + 3.5 * mean ** 0.5
    per_expert = -(-int(want) // MAX_EXPERT_BLOCK)
    return -(-int(want / per_expert) // BF16_ROWS) * BF16_ROWS
COMBINE_PARTS = 2
RUN_CHUNK = 256


def _run_copies(src_ref, src_row, dst_ref, dst_row, n, sem, *, max_rows, wait):
    def piece(done, size):
        s0 = pl.multiple_of((src_row + done) * FRAME_ROWS, FRAME_ROWS)
        d0 = pl.multiple_of((dst_row + done) * FRAME_ROWS, FRAME_ROWS)
        cp = pltpu.make_async_copy(src_ref.at[pl.ds(s0, size * FRAME_ROWS), :],
                                   dst_ref.at[pl.ds(d0, size * FRAME_ROWS), :], sem)
        if wait:
            cp.wait()
        else:
            cp.start()

    chunk_bits = RUN_CHUNK.bit_length() - 1
    if max_rows >= RUN_CHUNK:
        def chunk_body(c, carry):
            piece(c * RUN_CHUNK, RUN_CHUNK)
            return carry

        lax.fori_loop(0, lax.shift_right_logical(n, chunk_bits), chunk_body, 0)
    for bit in reversed(range(min(max_rows.bit_length(), chunk_bits))):
        done = lax.shift_left(lax.shift_right_logical(n, bit + 1), bit + 1)

        @pl.when(lax.bitwise_and(lax.shift_right_logical(n, bit), 1) == 1)
        def _():
            piece(done, 1 << bit)


def _scatter_kernel(offs_ref, gbase_ref, padrow_ref, padlen_ref, nused_ref, pos_ref, x1_ref, n2g_ref,
                    xg_ref, xy2_ref, stage_ref, zero_ref, sems, fill_sem, *, t, nt, n_blocks):
    k = pl.program_id(0)
    sub = min(MOE_SUB, t)
    blk8 = zero_ref.shape[0]
    slot = lax.rem(k, 2)
    xy_ref = xy2_ref.at[slot]

    def wait_runs(s):
        pltpu.make_async_copy(xy2_ref.at[s], xg_ref.at[pl.ds(0, 2 * t * FRAME_ROWS), :],
                              sems.at[s]).wait()

    @pl.when(k >= 2)
    def _():
        wait_runs(slot)

    def sub_body(si, carry):
        r0 = pl.multiple_of(si * sub, sub)
        xn = _rms_norm(x1_ref[0, pl.ds(r0, sub), :], n2g_ref[...])
        _to_frame_major(stage_ref, 0, xn)

        def tok_body(ti, c):
            base = r0 + ti * TOKENS_PER_STEP
            for i in range(TOKENS_PER_STEP):
                row = pl.multiple_of((ti * TOKENS_PER_STEP + i) * FRAME_ROWS, FRAME_ROWS)
                v = stage_ref[pl.ds(row, FRAME_ROWS), :]
                p1 = pl.multiple_of(pos_ref[0, 0, base + i], FRAME_ROWS)
                p2 = pl.multiple_of(pos_ref[0, 0, t + base + i], FRAME_ROWS)
                xy_ref[pl.ds(p1, FRAME_ROWS), :] = v
                xy_ref[pl.ds(p2, FRAME_ROWS), :] = v
            return c

        lax.fori_loop(0, sub // TOKENS_PER_STEP, tok_body, 0)
        return carry

    lax.fori_loop(0, t // sub, sub_body, 0)

    def run_body(e, c):
        start = offs_ref[k, e]
        _run_copies(xy_ref, start, xg_ref, gbase_ref[k, e], offs_ref[k, e + 1] - start,
                    sems.at[slot], max_rows=t, wait=False)
        return c

    lax.fori_loop(0, N_EXPERTS, run_body, 0)

    def fills(wait):
        def pad_body(e, c):
            _run_copies(zero_ref, 0, xg_ref, padrow_ref[e], padlen_ref[e], fill_sem,
                        max_rows=blk8 // FRAME_ROWS - 1, wait=wait)
            return c

        def tail_body(b, c):
            cp = pltpu.make_async_copy(
                zero_ref, xg_ref.at[pl.ds(pl.multiple_of(b * blk8, blk8), blk8), :], fill_sem)
            if wait:
                cp.wait()
            else:
                cp.start()
            return c

        lax.fori_loop(0, N_EXPERTS, pad_body, 0)
        lax.fori_loop(nused_ref[0], n_blocks, tail_body, 0)

    @pl.when(k == nt - 1)
    def _():
        zero_ref[...] = jnp.zeros(zero_ref.shape, _F32)
        fills(wait=False)
        fills(wait=True)
        wait_runs(slot)
        if nt > 1:
            wait_runs(1 - slot)


def _scatter(x1, pos, n2g, offs, gbase, padrow, padlen, nused, *, n_blocks, blk):
    nt, t, d = x1.shape
    grid_spec = pltpu.PrefetchScalarGridSpec(
        num_scalar_prefetch=5,
        grid=(nt,),
        in_specs=[
            pl.BlockSpec((1, 1, 2 * t), lambda k, *_: (k, 0, 0), memory_space=pltpu.SMEM),
            pl.BlockSpec((1, t, d), lambda k, *_: (k, 0, 0)),
            _const_spec(n2g.shape),
        ],
        out_specs=pl.BlockSpec(memory_space=pl.ANY),
        scratch_shapes=[pltpu.VMEM((2, 2 * t * FRAME_ROWS, LANES), _F32),
                        pltpu.VMEM((min(MOE_SUB, t) * FRAME_ROWS, LANES), _F32),
                        pltpu.VMEM((blk * FRAME_ROWS, LANES), _F32),
                        pltpu.SemaphoreType.DMA((2,)),
                        pltpu.SemaphoreType.DMA(())],
    )
    return pl.pallas_call(
        functools.partial(_scatter_kernel, t=t, nt=nt, n_blocks=n_blocks),
        grid_spec=grid_spec,
        out_shape=jax.ShapeDtypeStruct((n_blocks * blk * FRAME_ROWS, LANES), _F32),
        compiler_params=pltpu.CompilerParams(
            dimension_semantics=("arbitrary",), vmem_limit_bytes=VMEM_LIMIT_BYTES),
        name=f"moe_scatter_t{t}",
    )(offs, gbase, padrow, padlen, nused, pos, x1, n2g)


def _experts_kernel(bexp_ref, nused_ref, x_ref, wg_ref, wu_ref, wd_ref, y_ref,
                    wg_s, wu_s, wd_s):
    b = pl.program_id(0)

    @pl.when((b == 0) | (bexp_ref[b] != bexp_ref[jnp.maximum(b - 1, 0)]))
    def _():
        for src, dst in ((wg_ref, wg_s), (wu_ref, wu_s), (wd_ref, wd_s)):
            dst[...] = pltpu.bitcast(src[0].astype(_BF16), jnp.uint32)

    @pl.when(b < nused_ref[0])
    def _():
        xb = _from_frame_major(x_ref, 0, x_ref.shape[0] // FRAME_ROWS).astype(_BF16)
        h = jax.nn.silu(_wdot(xb, wg_s[...])) * _wdot(xb, wu_s[...])
        _to_frame_major(y_ref, 0, _wdot(h.astype(_BF16), wd_s[...]))

    @pl.when(b >= nused_ref[0])
    def _():
        y_ref[...] = jnp.zeros(y_ref.shape, _F32)


def _experts(xg, bexp, nused, wg, wu, wd, *, blk):
    rows8 = xg.shape[0]
    blk8 = blk * FRAME_ROWS
    nb = rows8 // blk8
    d = D_MODEL
    grid_spec = pltpu.PrefetchScalarGridSpec(
        num_scalar_prefetch=2,
        grid=(nb,),
        in_specs=[
            pl.BlockSpec((blk8, LANES), lambda b, be, nu: (jnp.minimum(b, nu[0] - 1), 0)),
            pl.BlockSpec((1, d, D_EXPERT), lambda b, be, nu: (be[b], 0, 0)),
            pl.BlockSpec((1, d, D_EXPERT), lambda b, be, nu: (be[b], 0, 0)),
            pl.BlockSpec((1, D_EXPERT, d), lambda b, be, nu: (be[b], 0, 0)),
        ],
        out_specs=pl.BlockSpec((blk8, LANES), lambda b, be, nu: (b, 0)),
        scratch_shapes=[pltpu.VMEM((d // 2, D_EXPERT), jnp.uint32),
                        pltpu.VMEM((d // 2, D_EXPERT), jnp.uint32),
                        pltpu.VMEM((D_EXPERT // 2, d), jnp.uint32)],
    )
    return pl.pallas_call(
        _experts_kernel,
        grid_spec=grid_spec,
        out_shape=jax.ShapeDtypeStruct(xg.shape, _F32),
        compiler_params=pltpu.CompilerParams(
            dimension_semantics=("arbitrary",), vmem_limit_bytes=VMEM_LIMIT_BYTES),
        name=f"moe_experts_b{blk}",
    )(bexp, nused, xg, wg, wu, wd)


def _combine_kernel(offs_ref, gbase_ref, pos_ref, wts_ref, x1_ref, fg_ref, yg_ref, out_ref,
                    yl2_ref, stage_ref, sems, *, t, nt):
    k = pl.program_id(0)
    h = pl.program_id(1)
    th = t // COMBINE_PARTS
    sub = min(MOE_SUB, th)
    slot = lax.rem(k, 2)
    yl_ref = yl2_ref.at[slot]

    def start_runs(tile, s):
        def run_body(e, c):
            start = offs_ref[tile, e]
            _run_copies(yg_ref, gbase_ref[tile, e], yl2_ref.at[s], start,
                        offs_ref[tile, e + 1] - start, sems.at[s], max_rows=t, wait=False)
            return c

        lax.fori_loop(0, N_EXPERTS, run_body, 0)

    @pl.when(h == 0)
    def _():
        @pl.when(k == 0)
        def _():
            start_runs(0, 0)

        @pl.when(k + 1 < nt)
        def _():
            start_runs(k + 1, 1 - slot)

        pltpu.make_async_copy(yg_ref.at[pl.ds(0, 2 * t * FRAME_ROWS), :], yl_ref,
                              sems.at[slot]).wait()

    def sub_body(si, carry):
        r0 = pl.multiple_of(si * sub, sub)

        def tok_body(ti, c):
            base = h * th + r0 + ti * TOKENS_PER_STEP
            for i in range(TOKENS_PER_STEP):
                p1 = pl.multiple_of(pos_ref[0, 0, base + i], FRAME_ROWS)
                p2 = pl.multiple_of(pos_ref[0, 0, t + base + i], FRAME_ROWS)
                v = (wts_ref[0, 0, base + i] * yl_ref[pl.ds(p1, FRAME_ROWS), :]
                     + wts_ref[0, 0, t + base + i] * yl_ref[pl.ds(p2, FRAME_ROWS), :])
                row = pl.multiple_of((ti * TOKENS_PER_STEP + i) * FRAME_ROWS, FRAME_ROWS)
                stage_ref[pl.ds(row, FRAME_ROWS), :] = v
            return c

        lax.fori_loop(0, sub // TOKENS_PER_STEP, tok_body, 0)
        x2 = x1_ref[0, pl.ds(r0, sub), :] + _from_frame_major(stage_ref, 0, sub)
        out_ref[0, pl.ds(r0, sub), :] = _rms_norm(x2, fg_ref[...])
        return carry

    lax.fori_loop(0, th // sub, sub_body, 0)


def _combine(x1, pos, wts, fg, yg, offs, gbase):
    nt, t, d = x1.shape
    th = t // COMBINE_PARTS
    grid_spec = pltpu.PrefetchScalarGridSpec(
        num_scalar_prefetch=2,
        grid=(nt, COMBINE_PARTS),
        in_specs=[
            pl.BlockSpec((1, 1, 2 * t), lambda k, h, *_: (k, 0, 0), memory_space=pltpu.SMEM),
            pl.BlockSpec((1, 1, 2 * t), lambda k, h, *_: (k, 0, 0), memory_space=pltpu.SMEM),
            pl.BlockSpec((1, th, d), lambda k, h, *_: (k, h, 0)),
            _const_spec(fg.shape),
            pl.BlockSpec(memory_space=pl.ANY),
        ],
        out_specs=pl.BlockSpec((1, th, d), lambda k, h, *_: (k, h, 0)),
        scratch_shapes=[pltpu.VMEM((2, 2 * t * FRAME_ROWS, LANES), _F32),
                        pltpu.VMEM((min(MOE_SUB, th) * FRAME_ROWS, LANES), _F32),
                        pltpu.SemaphoreType.DMA((2,))],
    )
    return pl.pallas_call(
        functools.partial(_combine_kernel, t=t, nt=nt),
        grid_spec=grid_spec,
        out_shape=jax.ShapeDtypeStruct((nt, t, d), _F32),
        compiler_params=pltpu.CompilerParams(
            dimension_semantics=("arbitrary", "arbitrary"), vmem_limit_bytes=VMEM_LIMIT_BYTES),
        name=f"moe_combine_t{t}",
    )(offs, gbase, pos, wts, x1, fg, yg)


def _route_and_moe(x1, router_w, moe_w):
    n2g, fg, wg, wu, wd = moe_w
    nt, t, _ = x1.shape
    pos, wts, offs = _router(x1, *router_w)
    blk = _expert_block(2 * nt * t)
    cnt = offs[:, 1:N_EXPERTS + 1] - offs[:, :N_EXPERTS]
    tot = jnp.sum(cnt, axis=0)
    nblk = (tot + (blk - 1)) // blk
    blk_end = jnp.cumsum(nblk)
    gstart = (blk_end - nblk) * blk
    gbase = (gstart[None, :] + jnp.cumsum(cnt, axis=0) - cnt).astype(jnp.int32)
    padrow = (gstart + tot).astype(jnp.int32)
    padlen = (nblk * blk - tot).astype(jnp.int32)
    n_blocks = (2 * nt * t + blk - 1) // blk + N_EXPERTS
    bexp = jnp.minimum(jnp.sum(jnp.arange(n_blocks)[:, None] >= blk_end[None, :], axis=1),
                       N_EXPERTS - 1).astype(jnp.int32)
    nused = blk_end[N_EXPERTS - 1:].astype(jnp.int32)
    xg = _scatter(x1, pos, n2g, offs, gbase, padrow, padlen, nused, n_blocks=n_blocks, blk=blk)
    yg = _experts(xg, bexp, nused, wg, wu, wd, blk=blk)
    return _combine(x1, pos, wts, fg, yg, offs, gbase)


def kernel(x_prompt, x_sample, cache_conv_a, cache_conv_b, norm1_g, w_in, gate_b, conv_a_w, w_out_a,
           conv_b_w, conv_b_b, ln_b_g, ln_b_b, w_out_b, b_out_b, w_o, norm2_g, router_group_w,
           router_group_b, router_expert_w, router_expert_b, exp_w_gate, exp_w_up, exp_w_down,
           final_norm_g):
    d = D_MODEL
    mixer_w = (norm1_g, w_in[0].astype(_BF16), gate_b, conv_a_w[0], w_out_a[0].astype(_BF16),
               conv_b_w[0].reshape(K_B, FRAME_ROWS, LANES), conv_b_b.reshape(FRAME_ROWS, LANES),
               ln_b_g, ln_b_b, w_out_b[0].astype(_BF16), b_out_b, w_o[0].astype(_BF16))
    n_router = N_GROUPS + N_EXPERTS
    wr = jnp.concatenate([router_group_w[0].T, router_expert_w[0].T,
                          jnp.zeros((E_PAD - n_router, d), _F32)], axis=0).astype(_BF16)
    rb = jnp.concatenate([router_group_b[0], router_expert_b[0],
                          jnp.zeros((E_PAD - n_router,), _F32)]).reshape(E_PAD, 1)
    router_w = (norm2_g, wr, rb)
    moe_w = (norm2_g, final_norm_g.reshape(1, d), exp_w_gate[0], exp_w_up[0], exp_w_down[0])

    x1_p, na_p, nb_p = _mixer(x_prompt, None, mixer_w, nb=1, tt=512)
    x1_s, na_s, nb_s = _mixer(x_sample, (cache_conv_a, cache_conv_b), mixer_w, nb=16, tt=16)
    bs, ts, _ = x_sample.shape
    y_p = _route_and_moe(x1_p, router_w, moe_w)
    y_s = _route_and_moe(x1_s.reshape(1, bs * ts, d), router_w, moe_w)
    return (y_p, y_s.reshape(bs, ts, d), na_p, na_s, nb_p, nb_s)
```

```python
import functools

import jax
import jax.numpy as jnp
from jax import lax
from jax.experimental import pallas as pl
from jax.experimental.pallas import tpu as pltpu

D_MODEL = 1024
K_A = 3
K_B = 31
N_GROUPS = 4
EXPERTS_PER_GROUP = 8
N_EXPERTS = N_GROUPS * EXPERTS_PER_GROUP
D_EXPERT = D_MODEL // 2
RMS_EPS = 1e-6
LN_EPS = 1e-5

SUBLANES = 8
LANES = 128
FRAME_ROWS = D_MODEL // LANES
assert FRAME_ROWS == SUBLANES
HIST_A = SUBLANES
assert HIST_A >= K_A - 1
CONV_B_SLICE = 16
W_B_A, W_C_A, W_H_A, W_GLU_A, W_GLU_B, W_GATE_A, W_GATE_B = range(7)

VMEM_LIMIT_BYTES = 58 * 1024 * 1024

_F32 = jnp.float32
_BF16 = jnp.bfloat16


def _rms_norm(x, g):
    return x * lax.rsqrt(jnp.mean(x * x, axis=-1, keepdims=True) + RMS_EPS) * g


def _dot(a, b):
    return jnp.dot(a, b, preferred_element_type=_F32)


def _wdot(a, w_packed):
    return _dot(a, pltpu.bitcast(w_packed, _BF16))


def _to_frame_major(dst_ref, first_frame, x):
    n = x.shape[0]
    for j in range(FRAME_ROWS):
        dst_ref[pl.ds(first_frame * FRAME_ROWS + j, n, stride=FRAME_ROWS), :] = (
            x[:, j * LANES:(j + 1) * LANES])


def _from_frame_major(src_ref, first_frame, n):
    return jnp.concatenate(
        [src_ref[pl.ds(first_frame * FRAME_ROWS + j, n, stride=FRAME_ROWS), :]
         for j in range(FRAME_ROWS)], axis=-1)


def _mixer_kernel(*refs, nb, tt, nt, has_cache):
    if has_cache:
        x_ref, ca_ref, cb_ref = refs[:3]
        refs = refs[3:]
    else:
        x_ref = refs[0]
        refs = refs[1:]
    (n1g_ref, win_ref, gateb_ref, caw_ref, woa_ref, cbw_ref, cbb_ref, lng_ref, lnb_ref,
     wob_ref, bob_ref, wo_ref, x1_ref, newa_ref, newb_ref, sa_ref, sb_ref, cv_ref) = refs
    d = D_MODEL
    m = nb * tt
    hb = K_B - 1
    seq_frames = hb + tt
    t = pl.program_id(1)
    ch = CONV_B_SLICE
    assert tt % ch == 0 if nb == 1 else tt == ch

    @pl.when(t == 0)
    def _():
        if has_cache:
            sa_ref[:, HIST_A - (K_A - 1):HIST_A, :] = ca_ref[0]
            for s in range(nb):
                _to_frame_major(sb_ref, s * seq_frames, cb_ref[0, s])
        else:
            sa_ref[:, 0:HIST_A, :] = jnp.zeros((nb, HIST_A, d), _F32)
            for s in range(nb):
                sb_ref[pl.ds(s * seq_frames * FRAME_ROWS, hb * FRAME_ROWS), :] = (
                    jnp.zeros((hb * FRAME_ROWS, LANES), _F32))

    x = x_ref[...].reshape(m, d)
    xn = _rms_norm(x, n1g_ref[...]).astype(_BF16)

    def in_proj(block):
        return _dot(xn, win_ref[:, block * d:(block + 1) * d])

    u = in_proj(W_GLU_A) * jax.nn.sigmoid(in_proj(W_GLU_B))
    u3 = u.reshape(nb, tt, d)
    if tt >= hb:
        newb_ref[0] = u3[:, tt - hb:tt, :]
    else:
        newb_ref[0, :, 0:hb - tt, :] = cb_ref[0, :, tt:hb, :]
        newb_ref[0, :, hb - tt:hb, :] = u3
    for s in range(nb):
        _to_frame_major(sb_ref, s * seq_frames + hb, u[s * tt:(s + 1) * tt, :])

    for q in range(m // ch):
        src = q * ch if nb == 1 else q * seq_frames
        acc = None
        for k in range(K_B):
            win = sb_ref[pl.ds((src + k) * FRAME_ROWS, ch * FRAME_ROWS), :]
            term = win.reshape(ch, FRAME_ROWS, LANES) * cbw_ref[k]
            acc = term if acc is None else acc + term
        acc = acc + cbb_ref[...]
        cv_ref[pl.ds(q * ch * FRAME_ROWS, ch * FRAME_ROWS), :] = acc.reshape(ch * FRAME_ROWS, LANES)
    c_a, h_a, b_a, ga_in, gb_in = (in_proj(c) for c in (W_C_A, W_H_A, W_B_A, W_GATE_A, W_GATE_B))
    if nt > 1:
        sb_ref[pl.ds(0, hb * FRAME_ROWS), :] = sb_ref[pl.ds(tt * FRAME_ROWS, hb * FRAME_ROWS), :]

    z = c_a * h_a
    sa_ref[:, HIST_A:HIST_A + tt, :] = z.reshape(nb, tt, d)
    conv_a = None
    for k in range(K_A):
        lo = HIST_A - (K_A - 1) + k
        term = sa_ref[:, lo:lo + tt, :] * caw_ref[k:k + 1, :]
        conv_a = term if conv_a is None else conv_a + term
    newa_ref[0] = sa_ref[:, HIST_A + tt - (K_A - 1):HIST_A + tt, :]
    if nt > 1:
        sa_ref[:, HIST_A - (K_A - 1):HIST_A, :] = sa_ref[:, HIST_A + tt - (K_A - 1):HIST_A + tt, :]
    y_a = _dot((b_a * conv_a.reshape(m, d)).astype(_BF16), woa_ref[...])

    g_a = jax.nn.sigmoid(ga_in + gateb_ref[:, 0:d])
    g_b = jax.nn.sigmoid(gb_in + gateb_ref[:, d:2 * d])

    cb = _from_frame_major(cv_ref, 0, m)
    mu = jnp.mean(cb, axis=-1, keepdims=True)
    xc = cb - mu
    var = jnp.mean(xc * xc, axis=-1, keepdims=True)
    v_b = jax.nn.silu(xc * lax.rsqrt(var + LN_EPS) * lng_ref[...] + lnb_ref[...])
    y_b = _dot(v_b.astype(_BF16), wob_ref[...]) + bob_ref[...]

    mix = (g_a * y_a + g_b * y_b).astype(_BF16)
    x1 = x + _dot(mix, wo_ref[...])
    x1_ref[...] = x1.reshape(nb, tt, d)


def _const_spec(shape):
    zeros = (0,) * len(shape)
    return pl.BlockSpec(shape, lambda *_: zeros, pipeline_mode=pl.Buffered(1))


def _mixer(x, caches, weights, *, nb, tt):
    b, s, d = x.shape
    assert b % nb == 0 and s % tt == 0
    nt = s // tt
    has_cache = caches is not None
    assert (nt == 1) if has_cache else (nb == 1 and tt >= K_B - 1)
    in_specs = [pl.BlockSpec((nb, tt, d), lambda i, j: (i, j, 0))]
    args = [x]
    if has_cache:
        in_specs += [pl.BlockSpec((1, nb, K_A - 1, d), lambda i, j: (0, i, 0, 0)),
                     pl.BlockSpec((1, nb, K_B - 1, d), lambda i, j: (0, i, 0, 0))]
        args += list(caches)
    in_specs += [_const_spec(w.shape) for w in weights]
    args += list(weights)
    out_shape = (jax.ShapeDtypeStruct((b, s, d), _F32),
                 jax.ShapeDtypeStruct((1, b, K_A - 1, d), _F32),
                 jax.ShapeDtypeStruct((1, b, K_B - 1, d), _F32))
    out_specs = (pl.BlockSpec((nb, tt, d), lambda i, j: (i, j, 0)),
                 pl.BlockSpec((1, nb, K_A - 1, d), lambda i, j: (0, i, 0, 0)),
                 pl.BlockSpec((1, nb, K_B - 1, d), lambda i, j: (0, i, 0, 0)))
    return pl.pallas_call(
        functools.partial(_mixer_kernel, nb=nb, tt=tt, nt=nt, has_cache=has_cache),
        grid=(b // nb, nt),
        in_specs=in_specs,
        out_specs=out_specs,
        out_shape=out_shape,
        scratch_shapes=[pltpu.VMEM((nb, HIST_A + tt, d), _F32),
                        pltpu.VMEM((nb * (K_B - 1 + tt) * FRAME_ROWS, LANES), _F32),
                        pltpu.VMEM((nb * tt * FRAME_ROWS, LANES), _F32)],
        compiler_params=pltpu.CompilerParams(
            dimension_semantics=("arbitrary", "arbitrary"),
            vmem_limit_bytes=VMEM_LIMIT_BYTES),
        name="mixer_cache" if has_cache else "mixer_prompt",
    )(*args)


E_PAD = 48
PREFIX_CHUNK = 256
COUNT_SPLIT = 64


def _router_kernel(x1_ref, n2g_ref, wr_ref, rb_ref, pos_ref, wts_ref, offs_ref, *, t):
    assert t % PREFIX_CHUNK == 0 and 2 * t <= COUNT_SPLIT * 256
    xn = _rms_norm(x1_ref[0], n2g_ref[...]).astype(_BF16)
    lg = lax.dot_general(wr_ref[...], xn, (((1,), (1,)), ((), ())),
                         preferred_element_type=_F32) + rb_ref[...]
    row = [lg[r:r + 1, :] for r in range(N_GROUPS + N_EXPERTS)]

    grp = jnp.zeros((1, t), jnp.int32)
    best = row[0]
    for g in range(1, N_GROUPS):
        gt = row[g] > best
        grp = jnp.where(gt, g, grp)
        best = jnp.where(gt, row[g], best)
    denom = None
    for g in range(N_GROUPS):
        e = jnp.exp(row[g] - best)
        denom = e if denom is None else denom + e
    p_top = 1.0 / denom

    sel = []
    for j in range(EXPERTS_PER_GROUP):
        v = row[N_GROUPS + j]
        for g in range(1, N_GROUPS):
            v = jnp.where(grp == g, row[N_GROUPS + g * EXPERTS_PER_GROUP + j], v)
        sel.append(v)
    i1 = jnp.zeros((1, t), jnp.int32)
    v1 = sel[0]
    for j in range(1, EXPERTS_PER_GROUP):
        gt = sel[j] > v1
        i1 = jnp.where(gt, j, i1)
        v1 = jnp.where(gt, sel[j], v1)
    i2 = jnp.where(i1 == 0, 1, 0).astype(jnp.int32)
    v2 = jnp.where(i1 == 0, sel[1], sel[0])
    for j in range(1, EXPERTS_PER_GROUP):
        gt = (sel[j] > v2) & (i1 != j) & (i2 != j)
        i2 = jnp.where(gt, j, i2)
        v2 = jnp.where(gt, sel[j], v2)
    e2x = jnp.exp(v2 - v1)
    ssum = 1.0 + e2x
    w1 = (1.0 / ssum) * p_top
    w2 = (e2x / ssum) * p_top
    e1 = grp * EXPERTS_PER_GROUP + i1
    e2 = grp * EXPERTS_PER_GROUP + i2

    eidx = lax.broadcasted_iota(jnp.int32, (E_PAD, t), 0)
    oh1 = eidx == e1
    oh2 = eidx == e2
    hot = (oh1 | oh2).astype(_F32)
    s_i = lax.broadcasted_iota(jnp.int32, (PREFIX_CHUNK, PREFIX_CHUNK), 0)
    t_i = lax.broadcasted_iota(jnp.int32, (PREFIX_CHUNK, PREFIX_CHUNK), 1)
    before = (s_i < t_i).astype(_BF16)
    carry = jnp.zeros((E_PAD, 1), _F32)
    prefix = []
    for c in range(t // PREFIX_CHUNK):
        hc = hot[:, c * PREFIX_CHUNK:(c + 1) * PREFIX_CHUNK]
        prefix.append(_dot(hc.astype(_BF16), before) + carry)
        carry = carry + jnp.sum(hc, axis=1, keepdims=True)
    prefix = jnp.concatenate(prefix, axis=1)
    counts = jnp.broadcast_to(carry, (E_PAD, LANES))
    c_hi = jnp.floor(counts * (1.0 / COUNT_SPLIT))
    c_lo = counts - c_hi * COUNT_SPLIT
    a_i = lax.broadcasted_iota(jnp.int32, (E_PAD, E_PAD), 0)
    b_i = lax.broadcasted_iota(jnp.int32, (E_PAD, E_PAD), 1)
    lower = (b_i < a_i).astype(_BF16)
    offs = _dot(lower, c_hi.astype(_BF16)) * COUNT_SPLIT + _dot(lower, c_lo.astype(_BF16))
    offs_ref[0] = offs[:, 0:1].astype(jnp.int32)
    slot = offs[:, 0:1] + prefix
    pos1 = jnp.sum(jnp.where(oh1, slot, 0.0), axis=0, keepdims=True)
    pos2 = jnp.sum(jnp.where(oh2, slot, 0.0), axis=0, keepdims=True)
    pos_ref[0, :, 0:t] = (pos1 * FRAME_ROWS).astype(jnp.int32)
    pos_ref[0, :, t:2 * t] = (pos2 * FRAME_ROWS).astype(jnp.int32)
    wts_ref[0, :, 0:t] = w1
    wts_ref[0, :, t:2 * t] = w2


def _router(x1, n2g, wr, rb):
    nt, t, d = x1.shape
    pos, wts, offs = pl.pallas_call(
        functools.partial(_router_kernel, t=t),
        grid=(nt,),
        in_specs=[pl.BlockSpec((1, t, d), lambda k: (k, 0, 0)),
                  _const_spec(n2g.shape), _const_spec(wr.shape), _const_spec(rb.shape)],
        out_specs=(pl.BlockSpec((1, 1, 2 * t), lambda k: (k, 0, 0)),
                   pl.BlockSpec((1, 1, 2 * t), lambda k: (k, 0, 0)),
                   pl.BlockSpec((1, E_PAD, 1), lambda k: (k, 0, 0))),
        out_shape=(jax.ShapeDtypeStruct((nt, 1, 2 * t), jnp.int32),
                   jax.ShapeDtypeStruct((nt, 1, 2 * t), _F32),
                   jax.ShapeDtypeStruct((nt, E_PAD, 1), jnp.int32)),
        compiler_params=pltpu.CompilerParams(
            dimension_semantics=("arbitrary",), vmem_limit_bytes=VMEM_LIMIT_BYTES),
        name=f"router_t{t}",
    )(x1, n2g, wr, rb)
    return pos, wts, offs.reshape(nt, E_PAD)


MOE_SUB = 256
TOKENS_PER_STEP = 8
MAX_EXPERT_BLOCK = 448
BF16_ROWS = 2 * SUBLANES


def _expert_block(n_rows):
    mean = n_rows / N_EXPERTS
    want = mean + 3.5 * mean ** 0.5
    per_expert = -(-int(want) // MAX_EXPERT_BLOCK)
    return -(-int(want / per_expert) // BF16_ROWS) * BF16_ROWS
COMBINE_PARTS = 2
RUN_CHUNK = 256
SMALL_SORT_BLOCK = 64


def _run_copies(src_ref, src_row, dst_ref, dst_row, n, sem, *, max_rows, wait):
    def piece(done, size):
        s0 = pl.multiple_of((src_row + done) * FRAME_ROWS, FRAME_ROWS)
        d0 = pl.multiple_of((dst_row + done) * FRAME_ROWS, FRAME_ROWS)
        cp = pltpu.make_async_copy(src_ref.at[pl.ds(s0, size * FRAME_ROWS), :],
                                   dst_ref.at[pl.ds(d0, size * FRAME_ROWS), :], sem)
        if wait:
            cp.wait()
        else:
            cp.start()

    chunk_bits = RUN_CHUNK.bit_length() - 1
    if max_rows >= RUN_CHUNK:
        def chunk_body(c, carry):
            piece(c * RUN_CHUNK, RUN_CHUNK)
            return carry

        lax.fori_loop(0, lax.shift_right_logical(n, chunk_bits), chunk_body, 0)
    for bit in reversed(range(min(max_rows.bit_length(), chunk_bits))):
        done = lax.shift_left(lax.shift_right_logical(n, bit + 1), bit + 1)

        @pl.when(lax.bitwise_and(lax.shift_right_logical(n, bit), 1) == 1)
        def _():
            piece(done, 1 << bit)


def _scatter_kernel(*refs, t, nt, n_blocks, has_guest):
    if has_guest:
        (offs_ref, gbase_ref, padrow_ref, padlen_ref, nused_ref, goffs_ref, gdst_ref,
         pos_ref, x1_ref, n2g_ref, guest_ref,
         xg_ref, xy2_ref, stage_ref, zero_ref, sems, fill_sem) = refs
    else:
        (offs_ref, gbase_ref, padrow_ref, padlen_ref, nused_ref, pos_ref, x1_ref, n2g_ref,
         xg_ref, xy2_ref, stage_ref, zero_ref, sems, fill_sem) = refs
    k = pl.program_id(0)
    sub = min(MOE_SUB, t)
    blk8 = zero_ref.shape[0]
    slot = lax.rem(k, 2)
    xy_ref = xy2_ref.at[slot]

    def wait_runs(s):
        pltpu.make_async_copy(xy2_ref.at[s], xg_ref.at[pl.ds(0, 2 * t * FRAME_ROWS), :],
                              sems.at[s]).wait()

    @pl.when(k >= 2)
    def _():
        wait_runs(slot)

    def sub_body(si, carry):
        r0 = pl.multiple_of(si * sub, sub)
        xn = _rms_norm(x1_ref[0, pl.ds(r0, sub), :], n2g_ref[...])
        _to_frame_major(stage_ref, 0, xn)

        def tok_body(ti, c):
            base = r0 + ti * TOKENS_PER_STEP
            for i in range(TOKENS_PER_STEP):
                row = pl.multiple_of((ti * TOKENS_PER_STEP + i) * FRAME_ROWS, FRAME_ROWS)
                v = stage_ref[pl.ds(row, FRAME_ROWS), :]
                p1 = pl.multiple_of(pos_ref[0, 0, base + i], FRAME_ROWS)
                p2 = pl.multiple_of(pos_ref[0, 0, t + base + i], FRAME_ROWS)
                xy_ref[pl.ds(p1, FRAME_ROWS), :] = v
                xy_ref[pl.ds(p2, FRAME_ROWS), :] = v
            return c

        lax.fori_loop(0, sub // TOKENS_PER_STEP, tok_body, 0)
        return carry

    lax.fori_loop(0, t // sub, sub_body, 0)

    def run_body(e, c):
        start = offs_ref[k, e]
        _run_copies(xy_ref, start, xg_ref, gbase_ref[k, e], offs_ref[k, e + 1] - start,
                    sems.at[slot], max_rows=t, wait=False)
        return c

    lax.fori_loop(0, N_EXPERTS, run_body, 0)

    def fills(wait):
        def pad_body(e, c):
            _run_copies(zero_ref, 0, xg_ref, padrow_ref[e], padlen_ref[e], fill_sem,
                        max_rows=blk8 // FRAME_ROWS - 1, wait=wait)
            return c

        def tail_body(b, c):
            cp = pltpu.make_async_copy(
                zero_ref, xg_ref.at[pl.ds(pl.multiple_of(b * blk8, blk8), blk8), :], fill_sem)
            if wait:
                cp.wait()
            else:
                cp.start()
            return c

        def guest_body(e, c):
            start = goffs_ref[0, e]
            _run_copies(guest_ref, start, xg_ref, gdst_ref[0, e], goffs_ref[0, e + 1] - start,
                        fill_sem, max_rows=guest_ref.shape[0] // FRAME_ROWS, wait=wait)
            return c

        lax.fori_loop(0, N_EXPERTS, pad_body, 0)
        lax.fori_loop(nused_ref[0], n_blocks, tail_body, 0)
        if has_guest:
            lax.fori_loop(0, N_EXPERTS, guest_body, 0)

    @pl.when(k == nt - 1)
    def _():
        zero_ref[...] = jnp.zeros(zero_ref.shape, _F32)
        fills(wait=False)
        fills(wait=True)
        wait_runs(slot)
        if nt > 1:
            wait_runs(1 - slot)


def _scatter(x1, pos, n2g, offs, gbase, padrow, padlen, nused, *, n_blocks, blk, guest=None):
    nt, t, d = x1.shape
    prefetch = [offs, gbase, padrow, padlen, nused]
    in_specs = [
        pl.BlockSpec((1, 1, 2 * t), lambda k, *_: (k, 0, 0), memory_space=pltpu.SMEM),
        pl.BlockSpec((1, t, d), lambda k, *_: (k, 0, 0)),
        _const_spec(n2g.shape),
    ]
    args = [pos, x1, n2g]
    if guest is not None:
        prefetch += [guest[1], guest[2]]
        in_specs.append(pl.BlockSpec(memory_space=pl.ANY))
        args.append(guest[0])
    grid_spec = pltpu.PrefetchScalarGridSpec(
        num_scalar_prefetch=len(prefetch),
        grid=(nt,),
        in_specs=in_specs,
        out_specs=pl.BlockSpec(memory_space=pl.ANY),
        scratch_shapes=[pltpu.VMEM((2, 2 * t * FRAME_ROWS, LANES), _F32),
                        pltpu.VMEM((min(MOE_SUB, t) * FRAME_ROWS, LANES), _F32),
                        pltpu.VMEM((blk * FRAME_ROWS, LANES), _F32),
                        pltpu.SemaphoreType.DMA((2,)),
                        pltpu.SemaphoreType.DMA(())],
    )
    return pl.pallas_call(
        functools.partial(_scatter_kernel, t=t, nt=nt, n_blocks=n_blocks,
                          has_guest=guest is not None),
        grid_spec=grid_spec,
        out_shape=jax.ShapeDtypeStruct((n_blocks * blk * FRAME_ROWS, LANES), _F32),
        compiler_params=pltpu.CompilerParams(
            dimension_semantics=("arbitrary",), vmem_limit_bytes=VMEM_LIMIT_BYTES),
        name=f"moe_scatter_t{t}",
    )(*prefetch, *args)


def _experts_kernel(bexp_ref, nused_ref, x_ref, wg_ref, wu_ref, wd_ref, y_ref,
                    wg_s, wu_s, wd_s):
    b = pl.program_id(0)

    @pl.when((b == 0) | (bexp_ref[b] != bexp_ref[jnp.maximum(b - 1, 0)]))
    def _():
        for src, dst in ((wg_ref, wg_s), (wu_ref, wu_s), (wd_ref, wd_s)):
            dst[...] = pltpu.bitcast(src[0].astype(_BF16), jnp.uint32)

    @pl.when(b < nused_ref[0])
    def _():
        xb = _from_frame_major(x_ref, 0, x_ref.shape[0] // FRAME_ROWS).astype(_BF16)
        h = jax.nn.silu(_wdot(xb, wg_s[...])) * _wdot(xb, wu_s[...])
        _to_frame_major(y_ref, 0, _wdot(h.astype(_BF16), wd_s[...]))

    @pl.when(b >= nused_ref[0])
    def _():
        y_ref[...] = jnp.zeros(y_ref.shape, _F32)


def _experts(xg, bexp, nused, wg, wu, wd, *, blk):
    rows8 = xg.shape[0]
    blk8 = blk * FRAME_ROWS
    nb = rows8 // blk8
    d = D_MODEL
    grid_spec = pltpu.PrefetchScalarGridSpec(
        num_scalar_prefetch=2,
        grid=(nb,),
        in_specs=[
            pl.BlockSpec((blk8, LANES), lambda b, be, nu: (jnp.minimum(b, nu[0] - 1), 0)),
            pl.BlockSpec((1, d, D_EXPERT), lambda b, be, nu: (be[b], 0, 0)),
            pl.BlockSpec((1, d, D_EXPERT), lambda b, be, nu: (be[b], 0, 0)),
            pl.BlockSpec((1, D_EXPERT, d), lambda b, be, nu: (be[b], 0, 0)),
        ],
        out_specs=pl.BlockSpec((blk8, LANES), lambda b, be, nu: (b, 0)),
        scratch_shapes=[pltpu.VMEM((d // 2, D_EXPERT), jnp.uint32),
                        pltpu.VMEM((d // 2, D_EXPERT), jnp.uint32),
                        pltpu.VMEM((D_EXPERT // 2, d), jnp.uint32)],
    )
    return pl.pallas_call(
        _experts_kernel,
        grid_spec=grid_spec,
        out_shape=jax.ShapeDtypeStruct(xg.shape, _F32),
        compiler_params=pltpu.CompilerParams(
            dimension_semantics=("arbitrary",), vmem_limit_bytes=VMEM_LIMIT_BYTES),
        name=f"moe_experts_b{blk}",
    )(bexp, nused, xg, wg, wu, wd)


def _combine_kernel(offs_ref, gbase_ref, pos_ref, wts_ref, x1_ref, fg_ref, yg_ref, out_ref,
                    yl2_ref, stage_ref, sems, *, t, nt):
    k = pl.program_id(0)
    h = pl.program_id(1)
    th = t // COMBINE_PARTS
    sub = min(MOE_SUB, th)
    slot = lax.rem(k, 2)
    yl_ref = yl2_ref.at[slot]

    def start_runs(tile, s):
        def run_body(e, c):
            start = offs_ref[tile, e]
            _run_copies(yg_ref, gbase_ref[tile, e], yl2_ref.at[s], start,
                        offs_ref[tile, e + 1] - start, sems.at[s], max_rows=t, wait=False)
            return c

        lax.fori_loop(0, N_EXPERTS, run_body, 0)

    @pl.when(h == 0)
    def _():
        @pl.when(k == 0)
        def _():
            start_runs(0, 0)

        @pl.when(k + 1 < nt)
        def _():
            start_runs(k + 1, 1 - slot)

        pltpu.make_async_copy(yg_ref.at[pl.ds(0, 2 * t * FRAME_ROWS), :], yl_ref,
                              sems.at[slot]).wait()

    def sub_body(si, carry):
        r0 = pl.multiple_of(si * sub, sub)

        def tok_body(ti, c):
            base = h * th + r0 + ti * TOKENS_PER_STEP
            for i in range(TOKENS_PER_STEP):
                p1 = pl.multiple_of(pos_ref[0, 0, base + i], FRAME_ROWS)
                p2 = pl.multiple_of(pos_ref[0, 0, t + base + i], FRAME_ROWS)
                v = (wts_ref[0, 0, base + i] * yl_ref[pl.ds(p1, FRAME_ROWS), :]
                     + wts_ref[0, 0, t + base + i] * yl_ref[pl.ds(p2, FRAME_ROWS), :])
                row = pl.multiple_of((ti * TOKENS_PER_STEP + i) * FRAME_ROWS, FRAME_ROWS)
                stage_ref[pl.ds(row, FRAME_ROWS), :] = v
            return c

        lax.fori_loop(0, sub // TOKENS_PER_STEP, tok_body, 0)
        x2 = x1_ref[0, pl.ds(r0, sub), :] + _from_frame_major(stage_ref, 0, sub)
        out_ref[0, pl.ds(r0, sub), :] = _rms_norm(x2, fg_ref[...])
        return carry

    lax.fori_loop(0, th // sub, sub_body, 0)


def _combine(x1, pos, wts, fg, yg, offs, gbase):
    nt, t, d = x1.shape
    th = t // COMBINE_PARTS
    grid_spec = pltpu.PrefetchScalarGridSpec(
        num_scalar_prefetch=2,
        grid=(nt, COMBINE_PARTS),
        in_specs=[
            pl.BlockSpec((1, 1, 2 * t), lambda k, h, *_: (k, 0, 0), memory_space=pltpu.SMEM),
            pl.BlockSpec((1, 1, 2 * t), lambda k, h, *_: (k, 0, 0), memory_space=pltpu.SMEM),
            pl.BlockSpec((1, th, d), lambda k, h, *_: (k, h, 0)),
            _const_spec(fg.shape),
            pl.BlockSpec(memory_space=pl.ANY),
        ],
        out_specs=pl.BlockSpec((1, th, d), lambda k, h, *_: (k, h, 0)),
        scratch_shapes=[pltpu.VMEM((2, 2 * t * FRAME_ROWS, LANES), _F32),
                        pltpu.VMEM((min(MOE_SUB, th) * FRAME_ROWS, LANES), _F32),
                        pltpu.SemaphoreType.DMA((2,))],
    )
    return pl.pallas_call(
        functools.partial(_combine_kernel, t=t, nt=nt),
        grid_spec=grid_spec,
        out_shape=jax.ShapeDtypeStruct((nt, t, d), _F32),
        compiler_params=pltpu.CompilerParams(
            dimension_semantics=("arbitrary", "arbitrary"), vmem_limit_bytes=VMEM_LIMIT_BYTES),
        name=f"moe_combine_t{t}",
    )(offs, gbase, pos, wts, x1, fg, yg)


def _route_and_moe(x1, x1_small, router_w, moe_w):
    n2g, fg, wg, wu, wd = moe_w
    nt, t, _ = x1.shape
    _, ts, _ = x1_small.shape
    pos, wts, offs = _router(x1, *router_w)
    pos_s, wts_s, offs_s = _router(x1_small, *router_w)
    n_rows = 2 * (nt * t + ts)
    blk = _expert_block(n_rows)
    offs_all = jnp.concatenate([offs, offs_s], axis=0)
    cnt = offs_all[:, 1:N_EXPERTS + 1] - offs_all[:, :N_EXPERTS]
    tot = jnp.sum(cnt, axis=0)
    nblk = (tot + (blk - 1)) // blk
    blk_end = jnp.cumsum(nblk)
    gstart = (blk_end - nblk) * blk
    gbase_all = (gstart[None, :] + jnp.cumsum(cnt, axis=0) - cnt).astype(jnp.int32)
    gbase, gbase_s = gbase_all[:nt], gbase_all[nt:]
    padrow = (gstart + tot).astype(jnp.int32)
    padlen = (nblk * blk - tot).astype(jnp.int32)
    n_blocks = (n_rows + blk - 1) // blk + N_EXPERTS
    bexp = jnp.minimum(jnp.sum(jnp.arange(n_blocks)[:, None] >= blk_end[None, :], axis=1),
                       N_EXPERTS - 1).astype(jnp.int32)
    nused = blk_end[N_EXPERTS - 1:].astype(jnp.int32)

    assert (2 * ts) % SMALL_SORT_BLOCK == 0
    n_small = 2 * ts // SMALL_SORT_BLOCK
    none = jnp.zeros((N_EXPERTS,), jnp.int32)
    sorted_small = _scatter(x1_small, pos_s, n2g, offs_s, offs_s[:, :N_EXPERTS], none, none,
                            jnp.full((1,), n_small, jnp.int32),
                            n_blocks=n_small, blk=SMALL_SORT_BLOCK)
    xg = _scatter(x1, pos, n2g, offs, gbase, padrow, padlen, nused, n_blocks=n_blocks, blk=blk,
                  guest=(sorted_small, offs_s, gbase_s))
    yg = _experts(xg, bexp, nused, wg, wu, wd, blk=blk)
    return (_combine(x1, pos, wts, fg, yg, offs, gbase),
            _combine(x1_small, pos_s, wts_s, fg, yg, offs_s, gbase_s))


def kernel(x_prompt, x_sample, cache_conv_a, cache_conv_b, norm1_g, w_in, gate_b, conv_a_w, w_out_a,
           conv_b_w, conv_b_b, ln_b_g, ln_b_b, w_out_b, b_out_b, w_o, norm2_g, router_group_w,
           router_group_b, router_expert_w, router_expert_b, exp_w_gate, exp_w_up, exp_w_down,
           final_norm_g):
    d = D_MODEL
    mixer_w = (norm1_g, w_in[0].astype(_BF16), gate_b, conv_a_w[0], w_out_a[0].astype(_BF16),
               conv_b_w[0].reshape(K_B, FRAME_ROWS, LANES), conv_b_b.reshape(FRAME_ROWS, LANES),
               ln_b_g, ln_b_b, w_out_b[0].astype(_BF16), b_out_b, w_o[0].astype(_BF16))
    n_router = N_GROUPS + N_EXPERTS
    wr = jnp.concatenate([router_group_w[0].T, router_expert_w[0].T,
                          jnp.zeros((E_PAD - n_router, d), _F32)], axis=0).astype(_BF16)
    rb = jnp.concatenate([router_group_b[0], router_expert_b[0],
                          jnp.zeros((E_PAD - n_router,), _F32)]).reshape(E_PAD, 1)
    router_w = (norm2_g, wr, rb)
    moe_w = (norm2_g, final_norm_g.reshape(1, d), exp_w_gate[0], exp_w_up[0], exp_w_down[0])

    x1_p, na_p, nb_p = _mixer(x_prompt, None, mixer_w, nb=1, tt=512)
    x1_s, na_s, nb_s = _mixer(x_sample, (cache_conv_a, cache_conv_b), mixer_w, nb=16, tt=16)
    bs, ts, _ = x_sample.shape
    y_p, y_s = _route_and_moe(x1_p, x1_s.reshape(1, bs * ts, d), router_w, moe_w)
    return (y_p, y_s.reshape(bs, ts, d), na_p, na_s, nb_p, nb_s)
```

```python
import functools

import jax
import jax.numpy as jnp
from jax import lax
from jax.experimental import pallas as pl
from jax.experimental.pallas import tpu as pltpu

D_MODEL = 1024
K_A = 3
K_B = 31
N_GROUPS = 4
EXPERTS_PER_GROUP = 8
N_EXPERTS = N_GROUPS * EXPERTS_PER_GROUP
D_EXPERT = D_MODEL // 2
RMS_EPS = 1e-6
LN_EPS = 1e-5

SUBLANES = 8
LANES = 128
FRAME_ROWS = D_MODEL // LANES
assert FRAME_ROWS == SUBLANES
HIST_A = SUBLANES
assert HIST_A >= K_A - 1
CONV_B_SLICE = 16
W_B_A, W_C_A, W_H_A, W_GLU_A, W_GLU_B, W_GATE_A, W_GATE_B = range(7)

VMEM_LIMIT_BYTES = 58 * 1024 * 1024

_F32 = jnp.float32
_BF16 = jnp.bfloat16


def _rms_norm(x, g):
    return x * lax.rsqrt(jnp.mean(x * x, axis=-1, keepdims=True) + RMS_EPS) * g


def _dot(a, b):
    return jnp.dot(a, b, preferred_element_type=_F32)


def _wdot(a, w_packed):
    return _dot(a, pltpu.bitcast(w_packed, _BF16))


def _to_frame_major(dst_ref, first_frame, x):
    n = x.shape[0]
    for j in range(FRAME_ROWS):
        dst_ref[pl.ds(first_frame * FRAME_ROWS + j, n, stride=FRAME_ROWS), :] = (
            x[:, j * LANES:(j + 1) * LANES])


def _from_frame_major(src_ref, first_frame, n):
    return jnp.concatenate(
        [src_ref[pl.ds(first_frame * FRAME_ROWS + j, n, stride=FRAME_ROWS), :]
         for j in range(FRAME_ROWS)], axis=-1)


def _mixer_kernel(*refs, nb, tt, nt, has_cache):
    if has_cache:
        x_ref, ca_ref, cb_ref = refs[:3]
        refs = refs[3:]
    else:
        x_ref = refs[0]
        refs = refs[1:]
    (n1g_ref, win_ref, gateb_ref, caw_ref, woa_ref, cbw_ref, cbb_ref, lng_ref, lnb_ref,
     wob_ref, bob_ref, wo_ref, x1_ref, newa_ref, newb_ref, sa_ref, sb_ref, cv_ref) = refs
    d = D_MODEL
    m = nb * tt
    hb = K_B - 1
    seq_frames = hb + tt
    t = pl.program_id(1)
    ch = CONV_B_SLICE
    assert tt % ch == 0 if nb == 1 else tt == ch

    @pl.when(t == 0)
    def _():
        if has_cache:
            sa_ref[:, HIST_A - (K_A - 1):HIST_A, :] = ca_ref[0]
            for s in range(nb):
                _to_frame_major(sb_ref, s * seq_frames, cb_ref[0, s])
        else:
            sa_ref[:, 0:HIST_A, :] = jnp.zeros((nb, HIST_A, d), _F32)
            for s in range(nb):
                sb_ref[pl.ds(s * seq_frames * FRAME_ROWS, hb * FRAME_ROWS), :] = (
                    jnp.zeros((hb * FRAME_ROWS, LANES), _F32))

    x = x_ref[...].reshape(m, d)
    xn = _rms_norm(x, n1g_ref[...]).astype(_BF16)

    def in_proj(block):
        return _dot(xn, win_ref[:, block * d:(block + 1) * d])

    u = in_proj(W_GLU_A) * jax.nn.sigmoid(in_proj(W_GLU_B))
    u3 = u.reshape(nb, tt, d)
    if tt >= hb:
        newb_ref[0] = u3[:, tt - hb:tt, :]
    else:
        newb_ref[0, :, 0:hb - tt, :] = cb_ref[0, :, tt:hb, :]
        newb_ref[0, :, hb - tt:hb, :] = u3
    for s in range(nb):
        _to_frame_major(sb_ref, s * seq_frames + hb, u[s * tt:(s + 1) * tt, :])

    for q in range(m // ch):
        src = q * ch if nb == 1 else q * seq_frames
        acc = None
        for k in range(K_B):
            win = sb_ref[pl.ds((src + k) * FRAME_ROWS, ch * FRAME_ROWS), :]
            term = win.reshape(ch, FRAME_ROWS, LANES) * cbw_ref[k]
            acc = term if acc is None else acc + term
        acc = acc + cbb_ref[...]
        cv_ref[pl.ds(q * ch * FRAME_ROWS, ch * FRAME_ROWS), :] = acc.reshape(ch * FRAME_ROWS, LANES)
    c_a, h_a, b_a, ga_in, gb_in = (in_proj(c) for c in (W_C_A, W_H_A, W_B_A, W_GATE_A, W_GATE_B))
    if nt > 1:
        sb_ref[pl.ds(0, hb * FRAME_ROWS), :] = sb_ref[pl.ds(tt * FRAME_ROWS, hb * FRAME_ROWS), :]

    z = c_a * h_a
    sa_ref[:, HIST_A:HIST_A + tt, :] = z.reshape(nb, tt, d)
    conv_a = None
    for k in range(K_A):
        lo = HIST_A - (K_A - 1) + k
        term = sa_ref[:, lo:lo + tt, :] * caw_ref[k:k + 1, :]
        conv_a = term if conv_a is None else conv_a + term
    newa_ref[0] = sa_ref[:, HIST_A + tt - (K_A - 1):HIST_A + tt, :]
    if nt > 1:
        sa_ref[:, HIST_A - (K_A - 1):HIST_A, :] = sa_ref[:, HIST_A + tt - (K_A - 1):HIST_A + tt, :]
    y_a = _dot((b_a * conv_a.reshape(m, d)).astype(_BF16), woa_ref[...])

    g_a = jax.nn.sigmoid(ga_in + gateb_ref[:, 0:d])
    g_b = jax.nn.sigmoid(gb_in + gateb_ref[:, d:2 * d])

    cb = _from_frame_major(cv_ref, 0, m)
    mu = jnp.mean(cb, axis=-1, keepdims=True)
    xc = cb - mu
    var = jnp.mean(xc * xc, axis=-1, keepdims=True)
    v_b = jax.nn.silu(xc * lax.rsqrt(var + LN_EPS) * lng_ref[...] + lnb_ref[...])
    y_b = _dot(v_b.astype(_BF16), wob_ref[...]) + bob_ref[...]

    mix = (g_a * y_a + g_b * y_b).astype(_BF16)
    x1 = x + _dot(mix, wo_ref[...])
    x1_ref[...] = x1.reshape(nb, tt, d)


def _const_spec(shape):
    zeros = (0,) * len(shape)
    return pl.BlockSpec(shape, lambda *_: zeros, pipeline_mode=pl.Buffered(1))


def _mixer(x, caches, weights, *, nb, tt):
    b, s, d = x.shape
    assert b % nb == 0 and s % tt == 0
    nt = s // tt
    has_cache = caches is not None
    assert (nt == 1) if has_cache else (nb == 1 and tt >= K_B - 1)
    in_specs = [pl.BlockSpec((nb, tt, d), lambda i, j: (i, j, 0))]
    args = [x]
    if has_cache:
        in_specs += [pl.BlockSpec((1, nb, K_A - 1, d), lambda i, j: (0, i, 0, 0)),
                     pl.BlockSpec((1, nb, K_B - 1, d), lambda i, j: (0, i, 0, 0))]
        args += list(caches)
    in_specs += [_const_spec(w.shape) for w in weights]
    args += list(weights)
    out_shape = (jax.ShapeDtypeStruct((b, s, d), _F32),
                 jax.ShapeDtypeStruct((1, b, K_A - 1, d), _F32),
                 jax.ShapeDtypeStruct((1, b, K_B - 1, d), _F32))
    out_specs = (pl.BlockSpec((nb, tt, d), lambda i, j: (i, j, 0)),
                 pl.BlockSpec((1, nb, K_A - 1, d), lambda i, j: (0, i, 0, 0)),
                 pl.BlockSpec((1, nb, K_B - 1, d), lambda i, j: (0, i, 0, 0)))
    return pl.pallas_call(
        functools.partial(_mixer_kernel, nb=nb, tt=tt, nt=nt, has_cache=has_cache),
        grid=(b // nb, nt),
        in_specs=in_specs,
        out_specs=out_specs,
        out_shape=out_shape,
        scratch_shapes=[pltpu.VMEM((nb, HIST_A + tt, d), _F32),
                        pltpu.VMEM((nb * (K_B - 1 + tt) * FRAME_ROWS, LANES), _F32),
                        pltpu.VMEM((nb * tt * FRAME_ROWS, LANES), _F32)],
        compiler_params=pltpu.CompilerParams(
            dimension_semantics=("arbitrary", "arbitrary"),
            vmem_limit_bytes=VMEM_LIMIT_BYTES),
        name="mixer_cache" if has_cache else "mixer_prompt",
    )(*args)


E_PAD = 48
PREFIX_CHUNK = 256
COUNT_SPLIT = 64


def _router_kernel(x1_ref, n2g_ref, wr_ref, rb_ref, pos_ref, wts_ref, offs_ref, *, t):
    assert t % PREFIX_CHUNK == 0 and 2 * t <= COUNT_SPLIT * 256
    xn = _rms_norm(x1_ref[0], n2g_ref[...]).astype(_BF16)
    lg = lax.dot_general(wr_ref[...], xn, (((1,), (1,)), ((), ())),
                         preferred_element_type=_F32) + rb_ref[...]
    row = [lg[r:r + 1, :] for r in range(N_GROUPS + N_EXPERTS)]

    grp = jnp.zeros((1, t), jnp.int32)
    best = row[0]
    for g in range(1, N_GROUPS):
        gt = row[g] > best
        grp = jnp.where(gt, g, grp)
        best = jnp.where(gt, row[g], best)
    denom = None
    for g in range(N_GROUPS):
        e = jnp.exp(row[g] - best)
        denom = e if denom is None else denom + e
    p_top = 1.0 / denom

    sel = []
    for j in range(EXPERTS_PER_GROUP):
        v = row[N_GROUPS + j]
        for g in range(1, N_GROUPS):
            v = jnp.where(grp == g, row[N_GROUPS + g * EXPERTS_PER_GROUP + j], v)
        sel.append(v)
    i1 = jnp.zeros((1, t), jnp.int32)
    v1 = sel[0]
    for j in range(1, EXPERTS_PER_GROUP):
        gt = sel[j] > v1
        i1 = jnp.where(gt, j, i1)
        v1 = jnp.where(gt, sel[j], v1)
    i2 = jnp.where(i1 == 0, 1, 0).astype(jnp.int32)
    v2 = jnp.where(i1 == 0, sel[1], sel[0])
    for j in range(1, EXPERTS_PER_GROUP):
        gt = (sel[j] > v2) & (i1 != j) & (i2 != j)
        i2 = jnp.where(gt, j, i2)
        v2 = jnp.where(gt, sel[j], v2)
    e2x = jnp.exp(v2 - v1)
    ssum = 1.0 + e2x
    w1 = (1.0 / ssum) * p_top
    w2 = (e2x / ssum) * p_top
    e1 = grp * EXPERTS_PER_GROUP + i1
    e2 = grp * EXPERTS_PER_GROUP + i2

    eidx = lax.broadcasted_iota(jnp.int32, (E_PAD, t), 0)
    oh1 = eidx == e1
    oh2 = eidx == e2
    hot = (oh1 | oh2).astype(_F32)
    s_i = lax.broadcasted_iota(jnp.int32, (PREFIX_CHUNK, PREFIX_CHUNK), 0)
    t_i = lax.broadcasted_iota(jnp.int32, (PREFIX_CHUNK, PREFIX_CHUNK), 1)
    before = (s_i < t_i).astype(_BF16)
    carry = jnp.zeros((E_PAD, 1), _F32)
    prefix = []
    for c in range(t // PREFIX_CHUNK):
        hc = hot[:, c * PREFIX_CHUNK:(c + 1) * PREFIX_CHUNK]
        prefix.append(_dot(hc.astype(_BF16), before) + carry)
        carry = carry + jnp.sum(hc, axis=1, keepdims=True)
    prefix = jnp.concatenate(prefix, axis=1)
    counts = jnp.broadcast_to(carry, (E_PAD, LANES))
    c_hi = jnp.floor(counts * (1.0 / COUNT_SPLIT))
    c_lo = counts - c_hi * COUNT_SPLIT
    a_i = lax.broadcasted_iota(jnp.int32, (E_PAD, E_PAD), 0)
    b_i = lax.broadcasted_iota(jnp.int32, (E_PAD, E_PAD), 1)
    lower = (b_i < a_i).astype(_BF16)
    offs = _dot(lower, c_hi.astype(_BF16)) * COUNT_SPLIT + _dot(lower, c_lo.astype(_BF16))
    offs_ref[0] = offs[:, 0:1].astype(jnp.int32)
    slot = offs[:, 0:1] + prefix
    pos1 = jnp.sum(jnp.where(oh1, slot, 0.0), axis=0, keepdims=True)
    pos2 = jnp.sum(jnp.where(oh2, slot, 0.0), axis=0, keepdims=True)
    pos_ref[0, :, 0:t] = (pos1 * FRAME_ROWS).astype(jnp.int32)
    pos_ref[0, :, t:2 * t] = (pos2 * FRAME_ROWS).astype(jnp.int32)
    wts_ref[0, :, 0:t] = w1
    wts_ref[0, :, t:2 * t] = w2


def _router(x1, n2g, wr, rb):
    nt, t, d = x1.shape
    pos, wts, offs = pl.pallas_call(
        functools.partial(_router_kernel, t=t),
        grid=(nt,),
        in_specs=[pl.BlockSpec((1, t, d), lambda k: (k, 0, 0)),
                  _const_spec(n2g.shape), _const_spec(wr.shape), _const_spec(rb.shape)],
        out_specs=(pl.BlockSpec((1, 1, 2 * t), lambda k: (k, 0, 0)),
                   pl.BlockSpec((1, 1, 2 * t), lambda k: (k, 0, 0)),
                   pl.BlockSpec((1, E_PAD, 1), lambda k: (k, 0, 0))),
        out_shape=(jax.ShapeDtypeStruct((nt, 1, 2 * t), jnp.int32),
                   jax.ShapeDtypeStruct((nt, 1, 2 * t), _F32),
                   jax.ShapeDtypeStruct((nt, E_PAD, 1), jnp.int32)),
        compiler_params=pltpu.CompilerParams(
            dimension_semantics=("arbitrary",), vmem_limit_bytes=VMEM_LIMIT_BYTES),
        name=f"router_t{t}",
    )(x1, n2g, wr, rb)
    return pos, wts, offs.reshape(nt, E_PAD)


MOE_SUB = 256
TOKENS_PER_STEP = 8
MAX_EXPERT_BLOCK = 448
BF16_ROWS = 2 * SUBLANES


def _expert_block(n_rows):
    mean = n_rows / N_EXPERTS
    want = mean + 3.5 * mean ** 0.5
    per_expert = -(-int(want) // MAX_EXPERT_BLOCK)
    return -(-int(want / per_expert) // BF16_ROWS) * BF16_ROWS
COMBINE_PARTS = 2
RUN_CHUNK = 256
SMALL_SORT_BLOCK = 64


def _run_copies(src_ref, src_row, dst_ref, dst_row, n, sem, *, max_rows, wait):
    def piece(done, size):
        s0 = pl.multiple_of((src_row + done) * FRAME_ROWS, FRAME_ROWS)
        d0 = pl.multiple_of((dst_row + done) * FRAME_ROWS, FRAME_ROWS)
        cp = pltpu.make_async_copy(src_ref.at[pl.ds(s0, size * FRAME_ROWS), :],
                                   dst_ref.at[pl.ds(d0, size * FRAME_ROWS), :], sem)
        if wait:
            cp.wait()
        else:
            cp.start()

    chunk_bits = RUN_CHUNK.bit_length() - 1
    if max_rows >= RUN_CHUNK:
        def chunk_body(c, carry):
            piece(c * RUN_CHUNK, RUN_CHUNK)
            return carry

        lax.fori_loop(0, lax.shift_right_logical(n, chunk_bits), chunk_body, 0)
    for bit in reversed(range(min(max_rows.bit_length(), chunk_bits))):
        done = lax.shift_left(lax.shift_right_logical(n, bit + 1), bit + 1)

        @pl.when(lax.bitwise_and(lax.shift_right_logical(n, bit), 1) == 1)
        def _():
            piece(done, 1 << bit)


def _scatter_kernel(*refs, t, nt, n_blocks, has_guest):
    if has_guest:
        (offs_ref, gbase_ref, padrow_ref, padlen_ref, nused_ref, goffs_ref, gdst_ref,
         pos_ref, x1_ref, n2g_ref, guest_ref,
         xg_ref, xy2_ref, stage_ref, zero_ref, sems, fill_sem) = refs
    else:
        (offs_ref, gbase_ref, padrow_ref, padlen_ref, nused_ref, pos_ref, x1_ref, n2g_ref,
         xg_ref, xy2_ref, stage_ref, zero_ref, sems, fill_sem) = refs
    k = pl.program_id(0)
    sub = min(MOE_SUB, t)
    blk8 = zero_ref.shape[0]
    slot = lax.rem(k, 2)
    xy_ref = xy2_ref.at[slot]

    def wait_runs(s):
        pltpu.make_async_copy(xy2_ref.at[s], xg_ref.at[pl.ds(0, 2 * t * FRAME_ROWS), :],
                              sems.at[s]).wait()

    @pl.when(k >= 2)
    def _():
        wait_runs(slot)

    def sub_body(si, carry):
        r0 = pl.multiple_of(si * sub, sub)
        xn = _rms_norm(x1_ref[0, pl.ds(r0, sub), :], n2g_ref[...])
        _to_frame_major(stage_ref, 0, xn)

        def tok_body(ti, c):
            base = r0 + ti * TOKENS_PER_STEP
            for i in range(TOKENS_PER_STEP):
                row = pl.multiple_of((ti * TOKENS_PER_STEP + i) * FRAME_ROWS, FRAME_ROWS)
                v = stage_ref[pl.ds(row, FRAME_ROWS), :]
                p1 = pl.multiple_of(pos_ref[0, 0, base + i], FRAME_ROWS)
                p2 = pl.multiple_of(pos_ref[0, 0, t + base + i], FRAME_ROWS)
                xy_ref[pl.ds(p1, FRAME_ROWS), :] = v
                xy_ref[pl.ds(p2, FRAME_ROWS), :] = v
            return c

        lax.fori_loop(0, sub // TOKENS_PER_STEP, tok_body, 0)
        return carry

    lax.fori_loop(0, t // sub, sub_body, 0)

    def run_body(e, c):
        start = offs_ref[k, e]
        _run_copies(xy_ref, start, xg_ref, gbase_ref[k, e], offs_ref[k, e + 1] - start,
                    sems.at[slot], max_rows=t, wait=False)
        return c

    lax.fori_loop(0, N_EXPERTS, run_body, 0)

    def fills(wait):
        def pad_body(e, c):
            _run_copies(zero_ref, 0, xg_ref, padrow_ref[e], padlen_ref[e], fill_sem,
                        max_rows=blk8 // FRAME_ROWS - 1, wait=wait)
            return c

        def tail_body(b, c):
            cp = pltpu.make_async_copy(
                zero_ref, xg_ref.at[pl.ds(pl.multiple_of(b * blk8, blk8), blk8), :], fill_sem)
            if wait:
                cp.wait()
            else:
                cp.start()
            return c

        lax.fori_loop(0, N_EXPERTS, pad_body, 0)
        lax.fori_loop(nused_ref[0], n_blocks, tail_body, 0)

    def guest_copies():
        rows8 = guest_ref.shape[0]
        assert rows8 <= xy2_ref.shape[1]
        staged = xy2_ref.at[1 - slot]
        guest_sem = sems.at[2]
        load = pltpu.make_async_copy(guest_ref, staged.at[pl.ds(0, rows8), :], guest_sem)
        load.start()
        load.wait()

        def runs(wait):
            def guest_body(e, c):
                start = goffs_ref[0, e]
                _run_copies(staged, start, xg_ref, gdst_ref[0, e], goffs_ref[0, e + 1] - start,
                            guest_sem, max_rows=rows8 // FRAME_ROWS, wait=wait)
                return c

            lax.fori_loop(0, N_EXPERTS, guest_body, 0)

        runs(wait=False)
        runs(wait=True)

    @pl.when(k == nt - 1)
    def _():
        zero_ref[...] = jnp.zeros(zero_ref.shape, _F32)
        fills(wait=False)
        if nt > 1:
            wait_runs(1 - slot)
        if has_guest:
            guest_copies()
        fills(wait=True)
        wait_runs(slot)


def _scatter(x1, pos, n2g, offs, gbase, padrow, padlen, nused, *, n_blocks, blk, guest=None):
    nt, t, d = x1.shape
    prefetch = [offs, gbase, padrow, padlen, nused]
    in_specs = [
        pl.BlockSpec((1, 1, 2 * t), lambda k, *_: (k, 0, 0), memory_space=pltpu.SMEM),
        pl.BlockSpec((1, t, d), lambda k, *_: (k, 0, 0)),
        _const_spec(n2g.shape),
    ]
    args = [pos, x1, n2g]
    if guest is not None:
        prefetch += [guest[1], guest[2]]
        in_specs.append(pl.BlockSpec(memory_space=pl.ANY))
        args.append(guest[0])
    grid_spec = pltpu.PrefetchScalarGridSpec(
        num_scalar_prefetch=len(prefetch),
        grid=(nt,),
        in_specs=in_specs,
        out_specs=pl.BlockSpec(memory_space=pl.ANY),
        scratch_shapes=[pltpu.VMEM((2, 2 * t * FRAME_ROWS, LANES), _F32),
                        pltpu.VMEM((min(MOE_SUB, t) * FRAME_ROWS, LANES), _F32),
                        pltpu.VMEM((blk * FRAME_ROWS, LANES), _F32),
                        pltpu.SemaphoreType.DMA((3,)),
                        pltpu.SemaphoreType.DMA(())],
    )
    return pl.pallas_call(
        functools.partial(_scatter_kernel, t=t, nt=nt, n_blocks=n_blocks,
                          has_guest=guest is not None),
        grid_spec=grid_spec,
        out_shape=jax.ShapeDtypeStruct((n_blocks * blk * FRAME_ROWS, LANES), _F32),
        compiler_params=pltpu.CompilerParams(
            dimension_semantics=("arbitrary",), vmem_limit_bytes=VMEM_LIMIT_BYTES),
        name=f"moe_scatter_t{t}",
    )(*prefetch, *args)


def _experts_kernel(bexp_ref, nused_ref, x_ref, wg_ref, wu_ref, wd_ref, y_ref,
                    wg_s, wu_s, wd_s):
    b = pl.program_id(0)

    @pl.when((b == 0) | (bexp_ref[b] != bexp_ref[jnp.maximum(b - 1, 0)]))
    def _():
        for src, dst in ((wg_ref, wg_s), (wu_ref, wu_s), (wd_ref, wd_s)):
            dst[...] = pltpu.bitcast(src[0].astype(_BF16), jnp.uint32)

    @pl.when(b < nused_ref[0])
    def _():
        xb = _from_frame_major(x_ref, 0, x_ref.shape[0] // FRAME_ROWS).astype(_BF16)
        h = jax.nn.silu(_wdot(xb, wg_s[...])) * _wdot(xb, wu_s[...])
        _to_frame_major(y_ref, 0, _wdot(h.astype(_BF16), wd_s[...]))

    @pl.when(b >= nused_ref[0])
    def _():
        y_ref[...] = jnp.zeros(y_ref.shape, _F32)


def _experts(xg, bexp, nused, wg, wu, wd, *, blk):
    rows8 = xg.shape[0]
    blk8 = blk * FRAME_ROWS
    nb = rows8 // blk8
    d = D_MODEL
    grid_spec = pltpu.PrefetchScalarGridSpec(
        num_scalar_prefetch=2,
        grid=(nb,),
        in_specs=[
            pl.BlockSpec((blk8, LANES), lambda b, be, nu: (jnp.minimum(b, nu[0] - 1), 0)),
            pl.BlockSpec((1, d, D_EXPERT), lambda b, be, nu: (be[b], 0, 0)),
            pl.BlockSpec((1, d, D_EXPERT), lambda b, be, nu: (be[b], 0, 0)),
            pl.BlockSpec((1, D_EXPERT, d), lambda b, be, nu: (be[b], 0, 0)),
        ],
        out_specs=pl.BlockSpec((blk8, LANES), lambda b, be, nu: (b, 0)),
        scratch_shapes=[pltpu.VMEM((d // 2, D_EXPERT), jnp.uint32),
                        pltpu.VMEM((d // 2, D_EXPERT), jnp.uint32),
                        pltpu.VMEM((D_EXPERT // 2, d), jnp.uint32)],
    )
    return pl.pallas_call(
        _experts_kernel,
        grid_spec=grid_spec,
        out_shape=jax.ShapeDtypeStruct(xg.shape, _F32),
        compiler_params=pltpu.CompilerParams(
            dimension_semantics=("arbitrary",), vmem_limit_bytes=VMEM_LIMIT_BYTES),
        name=f"moe_experts_b{blk}",
    )(bexp, nused, xg, wg, wu, wd)


def _combine_kernel(offs_ref, gbase_ref, pos_ref, wts_ref, x1_ref, fg_ref, yg_ref, out_ref,
                    yl2_ref, stage_ref, sems, *, t, nt):
    k = pl.program_id(0)
    h = pl.program_id(1)
    th = t // COMBINE_PARTS
    sub = min(MOE_SUB, th)
    slot = lax.rem(k, 2)
    yl_ref = yl2_ref.at[slot]

    def start_runs(tile, s):
        def run_body(e, c):
            start = offs_ref[tile, e]
            _run_copies(yg_ref, gbase_ref[tile, e], yl2_ref.at[s], start,
                        offs_ref[tile, e + 1] - start, sems.at[s], max_rows=t, wait=False)
            return c

        lax.fori_loop(0, N_EXPERTS, run_body, 0)

    @pl.when(h == 0)
    def _():
        @pl.when(k == 0)
        def _():
            start_runs(0, 0)

        @pl.when(k + 1 < nt)
        def _():
            start_runs(k + 1, 1 - slot)

        pltpu.make_async_copy(yg_ref.at[pl.ds(0, 2 * t * FRAME_ROWS), :], yl_ref,
                              sems.at[slot]).wait()

    def sub_body(si, carry):
        r0 = pl.multiple_of(si * sub, sub)

        def tok_body(ti, c):
            base = h * th + r0 + ti * TOKENS_PER_STEP
            for i in range(TOKENS_PER_STEP):
                p1 = pl.multiple_of(pos_ref[0, 0, base + i], FRAME_ROWS)
                p2 = pl.multiple_of(pos_ref[0, 0, t + base + i], FRAME_ROWS)
                v = (wts_ref[0, 0, base + i] * yl_ref[pl.ds(p1, FRAME_ROWS), :]
                     + wts_ref[0, 0, t + base + i] * yl_ref[pl.ds(p2, FRAME_ROWS), :])
                row = pl.multiple_of((ti * TOKENS_PER_STEP + i) * FRAME_ROWS, FRAME_ROWS)
                stage_ref[pl.ds(row, FRAME_ROWS), :] = v
            return c

        lax.fori_loop(0, sub // TOKENS_PER_STEP, tok_body, 0)
        x2 = x1_ref[0, pl.ds(r0, sub), :] + _from_frame_major(stage_ref, 0, sub)
        out_ref[0, pl.ds(r0, sub), :] = _rms_norm(x2, fg_ref[...])
        return carry

    lax.fori_loop(0, th // sub, sub_body, 0)


def _combine(x1, pos, wts, fg, yg, offs, gbase):
    nt, t, d = x1.shape
    th = t // COMBINE_PARTS
    grid_spec = pltpu.PrefetchScalarGridSpec(
        num_scalar_prefetch=2,
        grid=(nt, COMBINE_PARTS),
        in_specs=[
            pl.BlockSpec((1, 1, 2 * t), lambda k, h, *_: (k, 0, 0), memory_space=pltpu.SMEM),
            pl.BlockSpec((1, 1, 2 * t), lambda k, h, *_: (k, 0, 0), memory_space=pltpu.SMEM),
            pl.BlockSpec((1, th, d), lambda k, h, *_: (k, h, 0)),
            _const_spec(fg.shape),
            pl.BlockSpec(memory_space=pl.ANY),
        ],
        out_specs=pl.BlockSpec((1, th, d), lambda k, h, *_: (k, h, 0)),
        scratch_shapes=[pltpu.VMEM((2, 2 * t * FRAME_ROWS, LANES), _F32),
                        pltpu.VMEM((min(MOE_SUB, th) * FRAME_ROWS, LANES), _F32),
                        pltpu.SemaphoreType.DMA((2,))],
    )
    return pl.pallas_call(
        functools.partial(_combine_kernel, t=t, nt=nt),
        grid_spec=grid_spec,
        out_shape=jax.ShapeDtypeStruct((nt, t, d), _F32),
        compiler_params=pltpu.CompilerParams(
            dimension_semantics=("arbitrary", "arbitrary"), vmem_limit_bytes=VMEM_LIMIT_BYTES),
        name=f"moe_combine_t{t}",
    )(offs, gbase, pos, wts, x1, fg, yg)


def _route_and_moe(x1, x1_small, router_w, moe_w):
    n2g, fg, wg, wu, wd = moe_w
    nt, t, _ = x1.shape
    _, ts, _ = x1_small.shape
    pos, wts, offs = _router(x1, *router_w)
    pos_s, wts_s, offs_s = _router(x1_small, *router_w)
    n_rows = 2 * (nt * t + ts)
    blk = _expert_block(n_rows)
    offs_all = jnp.concatenate([offs, offs_s], axis=0)
    cnt = offs_all[:, 1:N_EXPERTS + 1] - offs_all[:, :N_EXPERTS]
    tot = jnp.sum(cnt, axis=0)
    nblk = (tot + (blk - 1)) // blk
    blk_end = jnp.cumsum(nblk)
    gstart = (blk_end - nblk) * blk
    gbase_all = (gstart[None, :] + jnp.cumsum(cnt, axis=0) - cnt).astype(jnp.int32)
    gbase, gbase_s = gbase_all[:nt], gbase_all[nt:]
    padrow = (gstart + tot).astype(jnp.int32)
    padlen = (nblk * blk - tot).astype(jnp.int32)
    n_blocks = (n_rows + blk - 1) // blk + N_EXPERTS
    bexp = jnp.minimum(jnp.sum(jnp.arange(n_blocks)[:, None] >= blk_end[None, :], axis=1),
                       N_EXPERTS - 1).astype(jnp.int32)
    nused = blk_end[N_EXPERTS - 1:].astype(jnp.int32)

    assert (2 * ts) % SMALL_SORT_BLOCK == 0
    n_small = 2 * ts // SMALL_SORT_BLOCK
    none = jnp.zeros((N_EXPERTS,), jnp.int32)
    sorted_small = _scatter(x1_small, pos_s, n2g, offs_s, offs_s[:, :N_EXPERTS], none, none,
                            jnp.full((1,), n_small, jnp.int32),
                            n_blocks=n_small, blk=SMALL_SORT_BLOCK)
    xg = _scatter(x1, pos, n2g, offs, gbase, padrow, padlen, nused, n_blocks=n_blocks, blk=blk,
                  guest=(sorted_small, offs_s, gbase_s))
    yg = _experts(xg, bexp, nused, wg, wu, wd, blk=blk)
    return (_combine(x1, pos, wts, fg, yg, offs, gbase),
            _combine(x1_small, pos_s, wts_s, fg, yg, offs_s, gbase_s))


def kernel(x_prompt, x_sample, cache_conv_a, cache_conv_b, norm1_g, w_in, gate_b, conv_a_w, w_out_a,
           conv_b_w, conv_b_b, ln_b_g, ln_b_b, w_out_b, b_out_b, w_o, norm2_g, router_group_w,
           router_group_b, router_expert_w, router_expert_b, exp_w_gate, exp_w_up, exp_w_down,
           final_norm_g):
    d = D_MODEL
    mixer_w = (norm1_g, w_in[0].astype(_BF16), gate_b, conv_a_w[0], w_out_a[0].astype(_BF16),
               conv_b_w[0].reshape(K_B, FRAME_ROWS, LANES), conv_b_b.reshape(FRAME_ROWS, LANES),
               ln_b_g, ln_b_b, w_out_b[0].astype(_BF16), b_out_b, w_o[0].astype(_BF16))
    n_router = N_GROUPS + N_EXPERTS
    wr = jnp.concatenate([router_group_w[0].T, router_expert_w[0].T,
                          jnp.zeros((E_PAD - n_router, d), _F32)], axis=0).astype(_BF16)
    rb = jnp.concatenate([router_group_b[0], router_expert_b[0],
                          jnp.zeros((E_PAD - n_router,), _F32)]).reshape(E_PAD, 1)
    router_w = (norm2_g, wr, rb)
    moe_w = (norm2_g, final_norm_g.reshape(1, d), exp_w_gate[0], exp_w_up[0], exp_w_down[0])

    x1_p, na_p, nb_p = _mixer(x_prompt, None, mixer_w, nb=1, tt=512)
    x1_s, na_s, nb_s = _mixer(x_sample, (cache_conv_a, cache_conv_b), mixer_w, nb=16, tt=16)
    bs, ts, _ = x_sample.shape
    y_p, y_s = _route_and_moe(x1_p, x1_s.reshape(1, bs * ts, d), router_w, moe_w)
    return (y_p, y_s.reshape(bs, ts, d), na_p, na_s, nb_p, nb_s)
```
